```python
import jax
import jax.numpy as jnp
from jax import lax
import numpy as np

D_MODEL = 2048
BATCH = 1
SEQ = 16384
DEPTH = 1

F32 = jnp.float32
EPS = 1e-5
SSD_EXPAND = 2
SSD_D_INNER = SSD_EXPAND * D_MODEL
SSD_HEAD_DIM = 64
SSD_N_HEADS = SSD_D_INNER // SSD_HEAD_DIM
SSD_N_GROUPS = 8
SSD_D_STATE = 128
SSD_CONV_WIDTH = 4
SSD_CHUNK = 256
SSD_CONV_DIM = SSD_D_INNER + 2 * SSD_N_GROUPS * SSD_D_STATE
ATT_HEAD_DIM = 128
ATT_N_HEADS = D_MODEL // ATT_HEAD_DIM
ATT_WIDTH = ATT_N_HEADS * ATT_HEAD_DIM
MOBA_BLOCK = 256
MOBA_TOPK = 3
MOBA_Q_CHUNK = 32
ROPE_THETA = 10000.0
N_BRANCHES = 2
IN_PROJ_DIM = SSD_D_INNER + SSD_CONV_DIM + SSD_N_HEADS + 3 * ATT_WIDTH + N_BRANCHES * D_MODEL
MEM_LEN = 256
XATT_N_HEADS = 4
XATT_HEAD_DIM = D_MODEL // XATT_N_HEADS
N_EXPERTS = 32
TOP_K = 4
D_FF = D_MODEL
SWIGLU_ALPHA = 1.702
SWIGLU_LIMIT = 7.0
MOE_BLOCK = 256

kernel_name = 'hybrid_ssd_moba_gated_moe_layer'


def rms_norm(x, w):
    xf = x.astype(F32)
    y = xf * lax.rsqrt(jnp.mean(xf * xf, axis=-1, keepdims=True) + EPS)
    return (y * w.astype(F32)).astype(x.dtype)


def pad_time(a, multiple):
    t = a.shape[1]
    tp = -(-t // multiple) * multiple
    if tp == t:
        return a
    pad = [(0, 0)] * a.ndim
    pad[1] = (0, tp - t)
    return jnp.pad(a, pad)


def rotary(u, positions):
    half = u.shape[-1] // 2
    inv_freq = ROPE_THETA ** (-jnp.arange(half, dtype=F32) / half)
    ang = positions.astype(F32)[..., None] * inv_freq
    cos = jnp.cos(ang)[:, :, None, :]
    sin = jnp.sin(ang)[:, :, None, :]
    uf = u.astype(F32)
    u1, u2 = uf[..., :half], uf[..., half:]
    return jnp.concatenate([u1 * cos - u2 * sin, u2 * cos + u1 * sin], axis=-1).astype(u.dtype)


def causal_depthwise_conv(u, w, bias):
    c = u.shape[-1]
    out = lax.conv_general_dilated(
        u, w[:, None, :].astype(u.dtype), window_strides=(1,),
        padding=[(SSD_CONV_WIDTH - 1, 0)], dimension_numbers=('NWC', 'WIO', 'NWC'),
        feature_group_count=c)
    return out + bias.astype(u.dtype)


def ssd_chunked_scan(x, dt, a, bm, cm):
    b, t, h, p = x.shape
    g, n = bm.shape[2], bm.shape[3]
    j = h // g
    q = SSD_CHUNK
    nc = t // q
    xr = jnp.moveaxis(x.reshape(b, nc, q, g, j, p), 1, 0)
    dtr = jnp.moveaxis(dt.reshape(b, nc, q, g, j), 1, 0)
    br = jnp.moveaxis(bm.reshape(b, nc, q, g, n), 1, 0)
    cr = jnp.moveaxis(cm.reshape(b, nc, q, g, n), 1, 0)
    ar = a.reshape(g, j)
    causal = jnp.tril(jnp.ones((q, q), dtype=bool))[None, :, :, None, None]

    def step(state, inp):
        xc, dtc, bc, cc = inp
        acum = jnp.cumsum(dtc * ar, axis=1)
        seg = acum[:, :, None] - acum[:, None, :]
        decay = jnp.exp(jnp.where(causal, seg, -jnp.inf))
        cb = jnp.einsum('blgn,bsgn->blsg', cc, bc)
        w = cb[..., None] * decay * dtc[:, None]
        y_diag = jnp.einsum('blsgj,bsgjp->blgjp', w, xc)
        y_off = jnp.einsum('blgn,bgjpn->blgjp', cc, state) * jnp.exp(acum)[..., None]
        to_end = jnp.exp(acum[:, -1:] - acum) * dtc
        new_state = (state * jnp.exp(acum[:, -1])[..., None, None]
                     + jnp.einsum('bsgn,bsgj,bsgjp->bgjpn', bc, to_end, xc))
        return new_state, y_diag + y_off

    state0 = jnp.zeros((b, g, j, p, n), F32)
    _, ys = lax.scan(step, state0, (xr, dtr, br, cr))
    return jnp.moveaxis(ys, 0, 1).reshape(b, t, h, p)


def gated_group_rms_norm(y, z, w):
    u = y.astype(F32) * jax.nn.silu(z.astype(F32))
    b, t, d = u.shape
    ug = u.reshape(b, t, SSD_N_GROUPS, d // SSD_N_GROUPS)
    ug = ug * lax.rsqrt(jnp.mean(ug * ug, axis=-1, keepdims=True) + EPS)
    return (ug.reshape(b, t, d) * w.astype(F32)).astype(z.dtype)


def moba_attention(q, k, v):
    b, h, t, dh = q.shape
    nb = t // MOBA_BLOCK
    kk = min(MOBA_TOPK, nb)
    scale = dh ** -0.5
    kb = k.reshape(b, h, nb, MOBA_BLOCK, dh)
    vb = v.reshape(b, h, nb, MOBA_BLOCK, dh)
    k_mean = jnp.mean(kb.astype(F32), axis=3)
    gate = jnp.einsum('bhtd,bhnd->bhtn', q.astype(F32), k_mean)
    q_blk = jnp.arange(t) // MOBA_BLOCK
    past = jnp.arange(nb)[None, :] < q_blk[:, None]
    gate = jnp.where(past, gate, -jnp.inf)
    _, top_idx = lax.top_k(gate, kk)
    valid = top_idx < q_blk[:, None]

    nc = t // MOBA_Q_CHUNK

    def to_chunks(u):
        u = u.reshape((b, h, nc, MOBA_Q_CHUNK) + u.shape[3:])
        return jnp.moveaxis(u, 2, 0)

    bi = jnp.arange(b)[:, None, None, None]
    hi = jnp.arange(h)[None, :, None, None]
    q_off = jnp.arange(MOBA_Q_CHUNK)
    k_off = jnp.arange(MOBA_BLOCK)

    def step(args):
        c, q_c, idx_c, val_c = args
        k_sel = kb[bi, hi, idx_c]
        v_sel = vb[bi, hi, idx_c]
        own = (c * MOBA_Q_CHUNK) // MOBA_BLOCK
        k_own = lax.dynamic_index_in_dim(kb, own, axis=2, keepdims=False)
        v_own = lax.dynamic_index_in_dim(vb, own, axis=2, keepdims=False)
        s_sel = jnp.einsum('bhqd,bhqjsd->bhqjs', q_c, k_sel).astype(F32) * scale
        s_sel = jnp.where(val_c[..., None], s_sel, -jnp.inf)
        s_own = jnp.einsum('bhqd,bhsd->bhqs', q_c, k_own).astype(F32) * scale
        q_pos = c * MOBA_Q_CHUNK + q_off
        k_pos = own * MOBA_BLOCK + k_off
        s_own = jnp.where(k_pos[None, :] <= q_pos[:, None], s_own, -jnp.inf)
        logits = jnp.concatenate([s_sel.reshape(b, h, MOBA_Q_CHUNK, kk * MOBA_BLOCK), s_own], axis=-1)
        prob = jax.nn.softmax(logits, axis=-1).astype(v.dtype)
        p_sel = prob[..., :kk * MOBA_BLOCK].reshape(b, h, MOBA_Q_CHUNK, kk, MOBA_BLOCK)
        p_own = prob[..., kk * MOBA_BLOCK:]
        return (jnp.einsum('bhqjs,bhqjsd->bhqd', p_sel, v_sel)
                + jnp.einsum('bhqs,bhsd->bhqd', p_own, v_own))

    out = lax.map(step, (jnp.arange(nc), to_chunks(q), to_chunks(top_idx), to_chunks(valid)))
    return jnp.moveaxis(out, 0, 2).reshape(b, h, t, dh)


def token_mix(h, positions, w_in, conv_w, conv_b, dt_bias, a_log, d_skip, ssd_norm_w,
              w_ssd_branch, w_moba_branch, w_out):
    b, t, _ = h.shape
    proj = h @ w_in
    sizes = [SSD_D_INNER, SSD_CONV_DIM, SSD_N_HEADS, ATT_WIDTH, ATT_WIDTH, ATT_WIDTH, D_MODEL]
    z, xbc, dt_raw, q, k, v, g_ssd, g_moba = jnp.split(proj, np.cumsum(sizes).tolist(), axis=-1)

    xbc = jax.nn.silu(causal_depthwise_conv(xbc, conv_w, conv_b))
    xs, bm, cm = jnp.split(xbc, [SSD_D_INNER, SSD_D_INNER + SSD_N_GROUPS * SSD_D_STATE], axis=-1)
    dt = jax.nn.softplus(dt_raw.astype(F32) + dt_bias.astype(F32))
    a = -jnp.exp(a_log.astype(F32))
    xs_h = xs.reshape(b, t, SSD_N_HEADS, SSD_HEAD_DIM).astype(F32)
    bm = bm.reshape(b, t, SSD_N_GROUPS, SSD_D_STATE).astype(F32)
    cm = cm.reshape(b, t, SSD_N_GROUPS, SSD_D_STATE).astype(F32)
    y = ssd_chunked_scan(pad_time(xs_h, SSD_CHUNK), pad_time(dt, SSD_CHUNK), a,
                         pad_time(bm, SSD_CHUNK), pad_time(cm, SSD_CHUNK))[:, :t]
    y = y + d_skip.astype(F32)[:, None] * xs_h
    y_ssd = gated_group_rms_norm(y.reshape(b, t, SSD_D_INNER), z, ssd_norm_w)

    qh = rotary(q.reshape(b, t, ATT_N_HEADS, ATT_HEAD_DIM), positions)
    kh = rotary(k.reshape(b, t, ATT_N_HEADS, ATT_HEAD_DIM), positions)
    vh = v.reshape(b, t, ATT_N_HEADS, ATT_HEAD_DIM)
    qh, kh, vh = (jnp.swapaxes(pad_time(u, MOBA_BLOCK), 1, 2) for u in (qh, kh, vh))
    y_moba = moba_attention(qh, kh, vh)[:, :, :t]
    y_moba = jnp.swapaxes(y_moba, 1, 2).reshape(b, t, ATT_WIDTH)

    mixed = (jax.nn.sigmoid(g_ssd) * (y_ssd @ w_ssd_branch)
             + jax.nn.sigmoid(g_moba) * (y_moba @ w_moba_branch))
    return mixed @ w_out


def memory_cross_attention(h, m, w_q, w_kv, w_o):
    b, t, _ = h.shape
    mlen = m.shape[1]
    q = (h @ w_q).reshape(b, t, XATT_N_HEADS, XATT_HEAD_DIM)
    k, v = jnp.split(m @ w_kv, 2, axis=-1)
    k = k.reshape(b, mlen, XATT_N_HEADS, XATT_HEAD_DIM)
    v = v.reshape(b, mlen, XATT_N_HEADS, XATT_HEAD_DIM)
    s = jnp.einsum('bthd,bmhd->bhtm', q, k).astype(F32) * (XATT_HEAD_DIM ** -0.5)
    prob = jax.nn.softmax(s, axis=-1).astype(v.dtype)
    o = jnp.einsum('bhtm,bmhd->bthd', prob, v).reshape(b, t, D_MODEL)
    return o @ w_o


def moe_ffn(h, w_router, b_router, w1, b1, w2, b2):
    b, t, d = h.shape
    n = b * t
    hf = h.reshape(n, d)
    logits = (hf @ w_router + b_router).astype(F32)
    top_val, top_idx = lax.top_k(logits, TOP_K)
    gates = jax.nn.softmax(top_val, axis=-1)
    e_flat = top_idx.reshape(-1)
    tok_flat = jnp.repeat(jnp.arange(n, dtype=jnp.int32), TOP_K)
    g_flat = gates.reshape(-1)
    order = jnp.argsort(e_flat)
    e_s, tok_s, g_s = e_flat[order], tok_flat[order], g_flat[order]
    counts = jnp.bincount(e_flat, length=N_EXPERTS)
    starts = jnp.cumsum(counts) - counts
    padded = (counts + MOE_BLOCK - 1) // MOE_BLOCK * MOE_BLOCK
    pad_end = jnp.cumsum(padded)
    pad_start = pad_end - padded
    dest = pad_start[e_s] + (jnp.arange(n * TOP_K) - starts[e_s])
    cap = (-(-(n * TOP_K) // MOE_BLOCK) + N_EXPERTS) * MOE_BLOCK
    n_blk = cap // MOE_BLOCK
    buf_tok = jnp.full((cap,), n, jnp.int32).at[dest].set(tok_s)
    blk_expert = jnp.minimum(
        jnp.searchsorted(pad_end, jnp.arange(n_blk) * MOE_BLOCK, side='right'), N_EXPERTS - 1)
    h_pad = jnp.concatenate([hf, jnp.zeros((1, d), hf.dtype)], axis=0)
    xb = h_pad[buf_tok].reshape(n_blk, MOE_BLOCK, d)

    def expert_block(args):
        e, xblk = args
        u = xblk @ w1[e] + b1[e]
        glu, lin = u[:, :D_FF], u[:, D_FF:]
        glu = jnp.minimum(glu, SWIGLU_LIMIT)
        lin = jnp.clip(lin, -SWIGLU_LIMIT, SWIGLU_LIMIT)
        act = glu * jax.nn.sigmoid(SWIGLU_ALPHA * glu) * (lin + 1.0)
        return act @ w2[e] + b2[e]

    yb = lax.map(expert_block, (blk_expert, xb)).reshape(cap, d)
    y_s = yb[dest] * g_s[:, None].astype(yb.dtype)
    out = jax.ops.segment_sum(y_s, tok_s, num_segments=n)
    return out.reshape(b, t, d)


def setup_inputs(seed: int = 0) -> dict:
    key = jax.random.key(seed)
    ks = jax.random.split(key, 28)
    L = DEPTH

    def nrm(k, shape, fan_in):
        return jax.random.normal(k, shape, F32) * (fan_in ** -0.5)

    def gain(k, shape):
        return 1.0 + 0.02 * jax.random.normal(k, shape, F32)

    def small(k, shape):
        return 0.01 * jax.random.normal(k, shape, F32)

    x = jax.random.normal(ks[0], (BATCH, SEQ, D_MODEL), F32)
    mem = jax.random.normal(ks[1], (BATCH, MEM_LEN, D_MODEL), F32)
    offset = jax.random.randint(ks[2], (BATCH, 1), 0, 1024, dtype=jnp.int32)
    positions = offset + jnp.arange(SEQ, dtype=jnp.int32)[None, :]
    dt0 = jnp.exp(jax.random.uniform(ks[6], (L, SSD_N_HEADS), F32, jnp.log(1e-3), jnp.log(1e-1)))
    dt_bias = dt0 + jnp.log(-jnp.expm1(-dt0))
    a_log = jnp.log(jax.random.uniform(ks[7], (L, SSD_N_HEADS), F32, 1.0, 16.0))
    return {
        'x': x,
        'mem': mem,
        'positions': positions,
        'norm_mix_w': gain(ks[3], (L, D_MODEL)),
        'w_in': nrm(ks[4], (L, D_MODEL, IN_PROJ_DIM), D_MODEL),
        'conv_w': nrm(ks[5], (L, SSD_CONV_WIDTH, SSD_CONV_DIM), SSD_CONV_WIDTH),
        'conv_b': small(ks[8], (L, SSD_CONV_DIM)),
        'dt_bias': dt_bias,
        'a_log': a_log,
        'd_skip': gain(ks[9], (L, SSD_N_HEADS)),
        'ssd_norm_w': gain(ks[10], (L, SSD_D_INNER)),
        'w_ssd_branch': nrm(ks[11], (L, SSD_D_INNER, D_MODEL), SSD_D_INNER),
        'w_moba_branch': nrm(ks[12], (L, ATT_WIDTH, D_MODEL), ATT_WIDTH),
        'w_out': nrm(ks[13], (L, D_MODEL, D_MODEL), D_MODEL),
        'norm_cross_w': gain(ks[14], (L, D_MODEL)),
        'norm_mem_w': gain(ks[15], (L, D_MODEL)),
        'w_xq': nrm(ks[16], (L, D_MODEL, D_MODEL), D_MODEL),
        'w_xkv': nrm(ks[17], (L, D_MODEL, 2 * D_MODEL), D_MODEL),
        'w_xo': nrm(ks[18], (L, D_MODEL, D_MODEL), D_MODEL),
        'norm_ffn_w': gain(ks[19], (L, D_MODEL)),
        'w_router': nrm(ks[20], (L, D_MODEL, N_EXPERTS), D_MODEL),
        'b_router': small(ks[21], (L, N_EXPERTS)),
        'w_mlp1': nrm(ks[22], (L, N_EXPERTS, D_MODEL, 2 * D_FF), D_MODEL),
        'b_mlp1': small(ks[23], (L, N_EXPERTS, 2 * D_FF)),
        'w_mlp2': nrm(ks[24], (L, N_EXPERTS, D_FF, D_MODEL), D_FF),
        'b_mlp2': small(ks[25], (L, N_EXPERTS, D_MODEL)),
        'norm_final_w': gain(ks[26], (D_MODEL,)),
    }


def reference(x, mem, positions, norm_mix_w, w_in, conv_w, conv_b, dt_bias, a_log, d_skip,
              ssd_norm_w, w_ssd_branch, w_moba_branch, w_out, norm_cross_w, norm_mem_w,
              w_xq, w_xkv, w_xo, norm_ffn_w, w_router, b_router, w_mlp1, b_mlp1, w_mlp2,
              b_mlp2, norm_final_w):
    for l in range(DEPTH):
        x = x + token_mix(rms_norm(x, norm_mix_w[l]), positions, w_in[l], conv_w[l], conv_b[l],
                          dt_bias[l], a_log[l], d_skip[l], ssd_norm_w[l], w_ssd_branch[l],
                          w_moba_branch[l], w_out[l])
        x = x + memory_cross_attention(rms_norm(x, norm_cross_w[l]), rms_norm(mem, norm_mem_w[l]),
                                       w_xq[l], w_xkv[l], w_xo[l])
        x = x + moe_ffn(rms_norm(x, norm_ffn_w[l]), w_router[l], b_router[l], w_mlp1[l],
                        b_mlp1[l], w_mlp2[l], b_mlp2[l])
    return rms_norm(x, norm_final_w)
```

```python
import functools

import jax
import jax.numpy as jnp
from jax import lax
from jax.experimental import pallas as pl
from jax.experimental.pallas import tpu as pltpu

F32 = jnp.float32
BF16 = jnp.bfloat16
I32 = jnp.int32

EPS = 1e-5
SSD_HEAD_DIM = 64
SSD_N_GROUPS = 8
SSD_D_STATE = 128
SSD_CONV_WIDTH = 4
SSD_CHUNK = 256
ATT_HEAD_DIM = 128
MOBA_BLOCK = 256
MOBA_TOPK = 3
ROPE_THETA = 10000.0
XATT_N_HEADS = 4
TOP_K = 4
SWIGLU_ALPHA = 1.702
SWIGLU_LIMIT = 7.0
MOE_BLOCK = 256

LANES = 128
VMEM_LIMIT = 56 * 1024 * 1024
MASKED = -1e30

NT_DIMS = (((1,), (1,)), ((), ()))


def _params(n_axes):
    return pltpu.CompilerParams(dimension_semantics=("arbitrary",) * n_axes,
                                vmem_limit_bytes=VMEM_LIMIT)


def _dot(a, b):
    return jnp.dot(a, b, preferred_element_type=F32)


def _dot_nt(a, b):
    return lax.dot_general(a, b, NT_DIMS, preferred_element_type=F32)


def _split3(v):
    hi = v.astype(BF16)
    r = v - hi.astype(F32)
    mid = r.astype(BF16)
    lo = (r - mid.astype(F32)).astype(BF16)
    return hi, mid, lo


def _dot3_rhs_exact(a, b_exact):
    hi, mid, lo = _split3(a)
    return _dot(hi, b_exact) + _dot(mid, b_exact) + _dot(lo, b_exact)


def _dot3_lhs_exact(a_exact, b):
    hi, mid, lo = _split3(b)
    return _dot(a_exact, hi) + _dot(a_exact, mid) + _dot(a_exact, lo)


def _rms(x, g):
    return x * lax.rsqrt(jnp.mean(x * x, axis=-1, keepdims=True) + EPS) * g


def _sigmoid(x):
    return 1.0 / (1.0 + jnp.exp(-x))


def _norm_matmul_kernel(x_ref, g_ref, w_ref, o_ref, h_ref):
    @pl.when(pl.program_id(1) == 0)
    def _():
        h_ref[...] = _rms(x_ref[...], g_ref[...]).astype(BF16)

    o_ref[...] = _dot(h_ref[...], w_ref[...]).astype(o_ref.dtype)


def norm_matmul(x, g, w, tm, tn, out_dtype):
    m, k = x.shape
    n = w.shape[1]
    return pl.pallas_call(
        _norm_matmul_kernel,
        out_shape=jax.ShapeDtypeStruct((m, n), out_dtype),
        grid=(m // tm, n // tn),
        in_specs=[pl.BlockSpec((tm, k), lambda i, j: (i, 0)),
                  pl.BlockSpec((1, k), lambda i, j: (0, 0)),
                  pl.BlockSpec((k, tn), lambda i, j: (0, j))],
        out_specs=pl.BlockSpec((tm, tn), lambda i, j: (i, j)),
        scratch_shapes=[pltpu.VMEM((tm, k), BF16)],
        compiler_params=_params(2),
        name="norm_matmul",
    )(x, g, w)


def _rope_table_kernel(pos_ref, freq_ref, sign_ref, cos_ref, sin_ref):
    ang = pos_ref[...] * freq_ref[...]
    cos_ref[...] = jnp.cos(ang)
    sin_ref[...] = jnp.sin(ang) * sign_ref[...]


def rope_tables(pos_col, freq2, sign2, tt):
    t = pos_col.shape[0]
    d = freq2.shape[1]
    out = jax.ShapeDtypeStruct((t, d), F32)
    return pl.pallas_call(
        _rope_table_kernel,
        out_shape=(out, out),
        grid=(t // tt,),
        in_specs=[pl.BlockSpec((tt, 1), lambda i: (i, 0)),
                  pl.BlockSpec((1, d), lambda i: (0, 0)),
                  pl.BlockSpec((1, d), lambda i: (0, 0))],
        out_specs=(pl.BlockSpec((tt, d), lambda i: (i, 0)),
                   pl.BlockSpec((tt, d), lambda i: (i, 0))),
        compiler_params=_params(1),
        name="rope_tables",
    )(pos_col, freq2, sign2)


def _conv_silu_kernel(u_ref, w_ref, b_ref, o_ref, ext_ref, *, tt):
    halo = 8

    @pl.when(pl.program_id(1) == 0)
    def _():
        ext_ref[0:halo, :] = jnp.zeros((halo, ext_ref.shape[1]), F32)

    ext_ref[halo:halo + tt, :] = u_ref[...]
    acc = b_ref[...]
    for i in range(SSD_CONV_WIDTH):
        off = halo - (SSD_CONV_WIDTH - 1) + i
        acc = acc + w_ref[i:i + 1, :] * ext_ref[off:off + tt, :]
    o_ref[...] = acc * _sigmoid(acc)
    ext_ref[0:halo, :] = ext_ref[tt:tt + halo, :]


def conv_silu(proj, col_off, width, conv_w, conv_b, tt, tc):
    t = proj.shape[0]
    off_blk = col_off // tc
    return pl.pallas_call(
        functools.partial(_conv_silu_kernel, tt=tt),
        out_shape=jax.ShapeDtypeStruct((t, width), F32),
        grid=(width // tc, t // tt),
        in_specs=[pl.BlockSpec((tt, tc), lambda j, i: (i, off_blk + j)),
                  pl.BlockSpec((SSD_CONV_WIDTH, tc), lambda j, i: (0, j)),
                  pl.BlockSpec((1, tc), lambda j, i: (0, j))],
        out_specs=pl.BlockSpec((tt, tc), lambda j, i: (i, j)),
        scratch_shapes=[pltpu.VMEM((tt + 8, tc), F32)],
        compiler_params=_params(2),
        name="conv_silu",
    )(proj, conv_w, conv_b)


def _ssd_kernel(x_ref, b_ref, c_ref, dt_ref, z_ref, bias_ref, alog_ref, dskip_ref, nw_ref,
                o_ref, state_ref, *, heads_per_group):
    chunk = x_ref.shape[0]
    gw = x_ref.shape[1]
    g = pl.program_id(1)

    @pl.when(pl.program_id(0) == 0)
    def _():
        state_ref[g] = jnp.zeros(state_ref.shape[1:], F32)

    raw = dt_ref[...] + bias_ref[...]
    dt = jnp.maximum(raw, 0.0) + jnp.log1p(jnp.exp(-jnp.abs(raw)))
    a = -jnp.exp(alog_ref[...])
    da = dt * a
    row = lax.broadcasted_iota(I32, (chunk, chunk), 0)
    col = lax.broadcasted_iota(I32, (chunk, chunk), 1)
    causal = row >= col
    tril = jnp.where(causal, 1.0, 0.0).astype(BF16)
    acum = _dot3_lhs_exact(tril, da)
    acum_t = acum.T
    a_last = acum[chunk - 1:chunk, :]
    exp_acum = jnp.exp(acum)
    to_end = jnp.exp(a_last - acum) * dt
    exp_last = jnp.broadcast_to(jnp.exp(a_last), (8, LANES))

    hsel = lax.broadcasted_iota(I32, (LANES, gw), 0)
    lane_head = lax.broadcasted_iota(I32, (LANES, gw), 1) // SSD_HEAD_DIM
    expand = jnp.where(hsel == lane_head, 1.0, 0.0).astype(BF16)
    stacked = jnp.concatenate([dt, to_end, exp_acum, exp_last], axis=0)
    spread = _dot3_rhs_exact(stacked, expand)
    dt_x = spread[0:chunk]
    to_end_x = spread[chunk:2 * chunk]
    exp_acum_x = spread[2 * chunk:3 * chunk]
    exp_last_x = spread[3 * chunk:3 * chunk + 1]

    x = x_ref[...]
    bmat = b_ref[...]
    cb16 = c_ref[...].astype(BF16)
    bb16 = bmat.astype(BF16)
    cb = _dot_nt(cb16, bb16)
    xdt = (x * dt_x).astype(BF16)
    state_t = state_ref[g]
    y_off = _dot(cb16, state_t.astype(BF16)) * exp_acum_x

    pieces = []
    for j in range(heads_per_group):
        seg = acum[:, j:j + 1] - acum_t[j:j + 1, :]
        decay = jnp.where(causal, jnp.exp(seg), 0.0)
        w = (cb * decay).astype(BF16)
        pieces.append(_dot(w, xdt[:, j * SSD_HEAD_DIM:(j + 1) * SSD_HEAD_DIM]))
    y_diag = jnp.concatenate(pieces, axis=1)

    bt16 = bmat.T.astype(BF16)
    state_ref[g] = state_t * exp_last_x + _dot(bt16, (x * to_end_x).astype(BF16))

    y = y_diag + y_off + dskip_ref[...] * x
    z = z_ref[...]
    u = y * (z * _sigmoid(z))
    u = u * lax.rsqrt(jnp.mean(u * u, axis=-1, keepdims=True) + EPS)
    o_ref[...] = (u * nw_ref[...]).astype(o_ref.dtype)


def ssd_scan(xbc_act, proj, dt_col_blk, bias_g, alog_g, dskip_x, norm_w, d_inner):
    t = xbc_act.shape[0]
    gw = d_inner // SSD_N_GROUPS
    heads_per_group = gw // SSD_HEAD_DIM
    b_blk = d_inner // SSD_D_STATE
    c_blk = b_blk + SSD_N_GROUPS
    return pl.pallas_call(
        functools.partial(_ssd_kernel, heads_per_group=heads_per_group),
        out_shape=jax.ShapeDtypeStruct((t, d_inner), BF16),
        grid=(t // SSD_CHUNK, SSD_N_GROUPS),
        in_specs=[pl.BlockSpec((SSD_CHUNK, gw), lambda c, g: (c, g)),
                  pl.BlockSpec((SSD_CHUNK, SSD_D_STATE), lambda c, g: (c, b_blk + g)),
                  pl.BlockSpec((SSD_CHUNK, SSD_D_STATE), lambda c, g: (c, c_blk + g)),
                  pl.BlockSpec((SSD_CHUNK, LANES), lambda c, g: (c, dt_col_blk + g)),
                  pl.BlockSpec((SSD_CHUNK, gw), lambda c, g: (c, g)),
                  pl.BlockSpec((None, 1, LANES), lambda c, g: (g, 0, 0)),
                  pl.BlockSpec((None, 1, LANES), lambda c, g: (g, 0, 0)),
                  pl.BlockSpec((1, gw), lambda c, g: (0, g)),
                  pl.BlockSpec((1, gw), lambda c, g: (0, g))],
        out_specs=pl.BlockSpec((SSD_CHUNK, gw), lambda c, g: (c, g)),
        scratch_shapes=[pltpu.VMEM((SSD_N_GROUPS, SSD_D_STATE, gw), F32)],
        compiler_params=_params(2),
        name="ssd_scan",
    )(xbc_act, xbc_act, xbc_act, proj, proj, bias_g, alog_g, dskip_x, norm_w)


def _moba_prep_kernel(q_ref, k_ref, v_ref, cos_ref, sin_ref, qo_ref, ko_ref, vt_ref, km_ref):
    cos = cos_ref[...]
    sin = sin_ref[...]
    half = ATT_HEAD_DIM // 2
    q = q_ref[...]
    k = k_ref[...]
    qr = q * cos + pltpu.roll(q, half, 1) * sin
    kr = k * cos + pltpu.roll(k, half, 1) * sin
    qo_ref[...] = qr.astype(BF16)
    n_blk = ko_ref.shape[0]
    kr3 = kr.reshape(n_blk, MOBA_BLOCK, ATT_HEAD_DIM)
    ko_ref[...] = kr3.astype(BF16)
    km_ref[...] = jnp.mean(kr3, axis=1)
    for b in range(n_blk):
        vt_ref[b] = v_ref[b * MOBA_BLOCK:(b + 1) * MOBA_BLOCK, :].T.astype(BF16)


def moba_prep(proj, q_blk, k_blk, v_blk, n_heads, cos2, sin2, tt):
    t = proj.shape[0]
    nb = t // MOBA_BLOCK
    bps = tt // MOBA_BLOCK
    d = ATT_HEAD_DIM
    return pl.pallas_call(
        _moba_prep_kernel,
        out_shape=(jax.ShapeDtypeStruct((n_heads, t, d), BF16),
                   jax.ShapeDtypeStruct((n_heads, nb, MOBA_BLOCK, d), BF16),
                   jax.ShapeDtypeStruct((n_heads, nb, d, MOBA_BLOCK), BF16),
                   jax.ShapeDtypeStruct((n_heads, nb, d), F32)),
        grid=(n_heads, t // tt),
        in_specs=[pl.BlockSpec((tt, d), lambda h, i: (i, q_blk + h)),
                  pl.BlockSpec((tt, d), lambda h, i: (i, k_blk + h)),
                  pl.BlockSpec((tt, d), lambda h, i: (i, v_blk + h)),
                  pl.BlockSpec((tt, d), lambda h, i: (i, 0)),
                  pl.BlockSpec((tt, d), lambda h, i: (i, 0))],
        out_specs=(pl.BlockSpec((None, tt, d), lambda h, i: (h, i, 0)),
                   pl.BlockSpec((None, bps, MOBA_BLOCK, d), lambda h, i: (h, i, 0, 0)),
                   pl.BlockSpec((None, bps, d, MOBA_BLOCK), lambda h, i: (h, i, 0, 0)),
                   pl.BlockSpec((None, bps, d), lambda h, i: (h, i, 0))),
        compiler_params=_params(2),
        name="moba_prep",
    )(proj, proj, proj, cos2, sin2)


def _moba_attn_kernel(q_ref, k_ref, vt_ref, km_ref, o_ref, bias_ref):
    i = pl.program_id(1)
    nb = k_ref.shape[0]
    q = q_ref[...]
    scale = ATT_HEAD_DIM ** -0.5

    gate = _dot_nt(km_ref[...].astype(BF16), q)
    blk = lax.broadcasted_iota(I32, gate.shape, 0)
    past = blk < i
    g = jnp.where(past, gate, -jnp.inf)
    chosen = jnp.zeros(gate.shape, F32)
    for _ in range(MOBA_TOPK):
        best = jnp.max(g, axis=0, keepdims=True)
        first = jnp.min(jnp.where(g == best, blk, nb), axis=0, keepdims=True)
        pick = blk == first
        chosen = jnp.where(pick, 1.0, chosen)
        g = jnp.where(pick, -jnp.inf, g)
    bias_ref[...] = jnp.where(past, jnp.where(chosen > 0.5, 0.0, MASKED), MASKED)

    def scores(j):
        return _dot_nt(k_ref[j], q) * scale

    key_pos = lax.broadcasted_iota(I32, (MOBA_BLOCK, MOBA_BLOCK), 0)
    qry_pos = lax.broadcasted_iota(I32, (MOBA_BLOCK, MOBA_BLOCK), 1)
    s = jnp.where(key_pos <= qry_pos, scores(i), MASKED)
    m = jnp.max(s, axis=0, keepdims=True)
    p = jnp.exp(s - m)
    l = jnp.sum(p, axis=0, keepdims=True)
    acc = _dot(vt_ref[i], p.astype(BF16))

    def body(j, carry):
        m, l, acc = carry
        s = scores(j) + bias_ref[pl.ds(j, 1), :]
        m_new = jnp.maximum(m, jnp.max(s, axis=0, keepdims=True))
        alpha = jnp.exp(m - m_new)
        p = jnp.exp(s - m_new)
        l = alpha * l + jnp.sum(p, axis=0, keepdims=True)
        acc = alpha * acc + _dot(vt_ref[j], p.astype(BF16))
        return m_new, l, acc

    m, l, acc = lax.fori_loop(0, i, body, (m, l, acc))
    o_ref[...] = (acc / l).T.astype(o_ref.dtype)


def moba_attention(q_rot, k_rot, v_t, k_mean):
    n_heads, t, d = q_rot.shape
    nb = t // MOBA_BLOCK
    return pl.pallas_call(
        _moba_attn_kernel,
        out_shape=jax.ShapeDtypeStruct((t, n_heads * d), BF16),
        grid=(n_heads, nb),
        in_specs=[pl.BlockSpec((None, MOBA_BLOCK, d), lambda h, i: (h, i, 0)),
                  pl.BlockSpec((None, nb, MOBA_BLOCK, d), lambda h, i: (h, 0, 0, 0)),
                  pl.BlockSpec((None, nb, d, MOBA_BLOCK), lambda h, i: (h, 0, 0, 0)),
                  pl.BlockSpec((None, nb, d), lambda h, i: (h, 0, 0))],
        out_specs=pl.BlockSpec((MOBA_BLOCK, d), lambda h, i: (i, h)),
        scratch_shapes=[pltpu.VMEM((nb, MOBA_BLOCK), F32)],
        compiler_params=_params(2),
        name="moba_attention",
    )(q_rot, k_rot, v_t, k_mean)


def _gated_merge_kernel(ys_ref, ym_ref, ws_ref, wm_ref, gs_ref, gm_ref, o_ref):
    a = _dot(ys_ref[...], ws_ref[...])
    b = _dot(ym_ref[...], wm_ref[...])
    o_ref[...] = (_sigmoid(gs_ref[...]) * a + _sigmoid(gm_ref[...]) * b).astype(o_ref.dtype)


def gated_merge(y_ssd, y_moba, w_s, w_m, proj, gate_col_off, tm, tn):
    t = y_ssd.shape[0]
    n = w_s.shape[1]
    gs_blk = gate_col_off // tn
    gm_blk = gs_blk + n // tn
    return pl.pallas_call(
        _gated_merge_kernel,
        out_shape=jax.ShapeDtypeStruct((t, n), BF16),
        grid=(t // tm, n // tn),
        in_specs=[pl.BlockSpec((tm, y_ssd.shape[1]), lambda i, j: (i, 0)),
                  pl.BlockSpec((tm, y_moba.shape[1]), lambda i, j: (i, 0)),
                  pl.BlockSpec((w_s.shape[0], tn), lambda i, j: (0, j)),
                  pl.BlockSpec((w_m.shape[0], tn), lambda i, j: (0, j)),
                  pl.BlockSpec((tm, tn), lambda i, j: (i, gs_blk + j)),
                  pl.BlockSpec((tm, tn), lambda i, j: (i, gm_blk + j))],
        out_specs=pl.BlockSpec((tm, tn), lambda i, j: (i, j)),
        compiler_params=_params(2),
        name="gated_merge",
    )(y_ssd, y_moba, w_s, w_m, proj, proj)


def _matmul_residual_kernel(a_ref, w_ref, r_ref, o_ref):
    o_ref[...] = r_ref[...] + _dot(a_ref[...], w_ref[...])


def matmul_residual(a, w, res, tm, tn):
    m, k = a.shape
    n = w.shape[1]
    return pl.pallas_call(
        _matmul_residual_kernel,
        out_shape=jax.ShapeDtypeStruct((m, n), F32),
        grid=(m // tm, n // tn),
        in_specs=[pl.BlockSpec((tm, k), lambda i, j: (i, 0)),
                  pl.BlockSpec((k, tn), lambda i, j: (0, j)),
                  pl.BlockSpec((tm, tn), lambda i, j: (i, j))],
        out_specs=pl.BlockSpec((tm, tn), lambda i, j: (i, j)),
        compiler_params=_params(2),
        name="matmul_residual",
    )(a, w, res)


def _cross_attn_kernel(x_ref, g_ref, wq_ref, kv_ref, wo_ref, o_ref):
    x = x_ref[...]
    d = x.shape[1]
    hd = d // XATT_N_HEADS
    h = _rms(x, g_ref[...]).astype(BF16)
    q = _dot(h, wq_ref[...]).astype(BF16)
    outs = []
    for hh in range(XATT_N_HEADS):
        qh = q[:, hh * hd:(hh + 1) * hd]
        kh = kv_ref[:, hh * hd:(hh + 1) * hd]
        vh = kv_ref[:, d + hh * hd:d + (hh + 1) * hd]
        s = _dot_nt(qh, kh) * (hd ** -0.5)
        p = jnp.exp(s - jnp.max(s, axis=-1, keepdims=True))
        p = p / jnp.sum(p, axis=-1, keepdims=True)
        outs.append(_dot(p.astype(BF16), vh))
    o = jnp.concatenate(outs, axis=1).astype(BF16)
    o_ref[...] = x + _dot(o, wo_ref[...])


def cross_attention(x, g, w_q, kv, w_o, tm):
    t, d = x.shape
    const = lambda i: (0, 0)
    single = pl.Buffered(1)
    return pl.pallas_call(
        _cross_attn_kernel,
        out_shape=jax.ShapeDtypeStruct((t, d), F32),
        grid=(t // tm,),
        in_specs=[pl.BlockSpec((tm, d), lambda i: (i, 0)),
                  pl.BlockSpec((1, d), const),
                  pl.BlockSpec(w_q.shape, const, pipeline_mode=single),
                  pl.BlockSpec(kv.shape, const, pipeline_mode=single),
                  pl.BlockSpec(w_o.shape, const, pipeline_mode=single)],
        out_specs=pl.BlockSpec((tm, d), lambda i: (i, 0)),
        compiler_params=_params(1),
        name="cross_attention",
    )(x, g, w_q, kv, w_o)


def _router_kernel(x_ref, g_ref, whi_ref, wlo_ref, b_ref, h_ref, idx_ref, gate_ref, rank_ref,
                   cnt_ref, run_ref):
    tm = x_ref.shape[0]
    n_exp = whi_ref.shape[0]

    @pl.when(pl.program_id(0) == 0)
    def _():
        run_ref[...] = jnp.zeros(run_ref.shape, F32)

    h = _rms(x_ref[...], g_ref[...])
    h_ref[...] = h
    hi = h.astype(BF16)
    lo = (h - hi.astype(F32)).astype(BF16)
    logits = (_dot_nt(whi_ref[...], hi) + _dot_nt(whi_ref[...], lo) + _dot_nt(wlo_ref[...], hi)
              + b_ref[...])

    eid = lax.broadcasted_iota(I32, logits.shape, 0)
    g = logits
    member = jnp.zeros(logits.shape, F32)
    idxs, vals, picks = [], [], []
    for _ in range(TOP_K):
        best = jnp.max(g, axis=0, keepdims=True)
        first = jnp.min(jnp.where(g == best, eid, n_exp), axis=0, keepdims=True)
        pick = eid == first
        member = jnp.where(pick, 1.0, member)
        g = jnp.where(pick, -jnp.inf, g)
        idxs.append(first)
        vals.append(best)
        picks.append(pick)
    exps = [jnp.exp(v - vals[0]) for v in vals]
    denom = exps[0]
    for e in exps[1:]:
        denom = denom + e

    src = lax.broadcasted_iota(I32, (tm, tm), 0)
    dst = lax.broadcasted_iota(I32, (tm, tm), 1)
    before = jnp.where(src < dst, 1.0, 0.0).astype(BF16)
    run = run_ref[...]
    rank_all = _dot(member.astype(BF16), before) + run[:, 0:1]
    ranks = [jnp.sum(jnp.where(p, rank_all, 0.0), axis=0, keepdims=True) for p in picks]

    idx_ref[...] = jnp.concatenate(idxs, axis=0)
    gate_ref[...] = jnp.concatenate([e / denom for e in exps], axis=0)
    rank_ref[...] = jnp.concatenate(ranks, axis=0).astype(I32)
    run = run + jnp.sum(member, axis=1, keepdims=True)
    run_ref[...] = run
    cnt_ref[...] = run.astype(I32)


def moe_router(x, g, w_hi, w_lo, b_col, tm):
    t, d = x.shape
    n_exp = w_hi.shape[0]
    const = lambda i: (0, 0)
    return pl.pallas_call(
        _router_kernel,
        out_shape=(jax.ShapeDtypeStruct((t, d), F32),
                   jax.ShapeDtypeStruct((TOP_K, t), I32),
                   jax.ShapeDtypeStruct((TOP_K, t), F32),
                   jax.ShapeDtypeStruct((TOP_K, t), I32),
                   jax.ShapeDtypeStruct((n_exp, LANES), I32)),
        grid=(t // tm,),
        in_specs=[pl.BlockSpec((tm, d), lambda i: (i, 0)),
                  pl.BlockSpec((1, d), const),
                  pl.BlockSpec((n_exp, d), const),
                  pl.BlockSpec((n_exp, d), const),
                  pl.BlockSpec((n_exp, 1), const)],
        out_specs=(pl.BlockSpec((tm, d), lambda i: (i, 0)),
                   pl.BlockSpec((TOP_K, tm), lambda i: (0, i)),
                   pl.BlockSpec((TOP_K, tm), lambda i: (0, i)),
                   pl.BlockSpec((TOP_K, tm), lambda i: (0, i)),
                   pl.BlockSpec((n_exp, LANES), const)),
        scratch_shapes=[pltpu.VMEM((n_exp, LANES), F32)],
        compiler_params=_params(1),
        name="moe_router",
    )(x, g, w_hi, w_lo, b_col)


def _dispatch_kernel(dest_ref, h_ref, buf_in_ref, buf_ref, sem):
    del buf_in_ref
    tt = h_ref.shape[0]

    def row_copy(r, slot):
        return pltpu.make_async_copy(h_ref.at[pl.ds(r, 1), :], buf_ref.at[pl.ds(slot, 1), :], sem)

    def start(r, carry):
        for k in range(TOP_K):
            row_copy(r, dest_ref[k, r]).start()
        return carry

    def wait(r, carry):
        for k in range(TOP_K):
            row_copy(r, dest_ref[k, r]).wait()
        return carry

    lax.fori_loop(0, tt, start, 0)
    lax.fori_loop(0, tt, wait, 0)


def moe_dispatch(dest, h, buf_zero, tt):
    t, d = h.shape
    return pl.pallas_call(
        _dispatch_kernel,
        out_shape=jax.ShapeDtypeStruct(buf_zero.shape, buf_zero.dtype),
        grid=(t // tt,),
        in_specs=[pl.BlockSpec((TOP_K, tt), lambda i: (0, i), memory_space=pltpu.SMEM),
                  pl.BlockSpec((tt, d), lambda i: (i, 0)),
                  pl.BlockSpec(memory_space=pl.ANY)],
        out_specs=pl.BlockSpec(memory_space=pl.ANY),
        scratch_shapes=[pltpu.SemaphoreType.DMA(())],
        input_output_aliases={2: 0},
        compiler_params=_params(1),
        name="moe_dispatch",
    )(dest, h, buf_zero)


def _expert_up_kernel(be_ref, x_ref, wg_ref, wl_ref, bg_ref, bl_ref, o_ref):
    del be_ref
    x = x_ref[...].astype(BF16)
    glu = _dot(x, wg_ref[...]) + bg_ref[...]
    lin = _dot(x, wl_ref[...]) + bl_ref[...]
    glu = jnp.minimum(glu, SWIGLU_LIMIT)
    lin = jnp.clip(lin, -SWIGLU_LIMIT, SWIGLU_LIMIT)
    o_ref[...] = (glu * _sigmoid(SWIGLU_ALPHA * glu) * (lin + 1.0)).astype(o_ref.dtype)


def expert_up(blk_expert, xb, w1, b1, tf):
    cap, d = xb.shape
    d_ff = w1.shape[2] // 2
    n_f = d_ff // tf
    n_blk = cap // MOE_BLOCK
    grid_spec = pltpu.PrefetchScalarGridSpec(
        num_scalar_prefetch=1,
        grid=(n_f, n_blk),
        in_specs=[pl.BlockSpec((MOE_BLOCK, d), lambda f, b, be: (b, 0)),
                  pl.BlockSpec((None, d, tf), lambda f, b, be: (be[b], 0, f)),
                  pl.BlockSpec((None, d, tf), lambda f, b, be: (be[b], 0, n_f + f)),
                  pl.BlockSpec((None, 1, tf), lambda f, b, be: (be[b], 0, f)),
                  pl.BlockSpec((None, 1, tf), lambda f, b, be: (be[b], 0, n_f + f))],
        out_specs=pl.BlockSpec((MOE_BLOCK, tf), lambda f, b, be: (b, f)),
    )
    return pl.pallas_call(
        _expert_up_kernel,
        out_shape=jax.ShapeDtypeStruct((cap, d_ff), BF16),
        grid_spec=grid_spec,
        compiler_params=_params(2),
        name="expert_up",
    )(blk_expert, xb, w1, w1, b1, b1)


def _expert_down_kernel(be_ref, a_ref, w_ref, b_ref, o_ref):
    del be_ref
    o_ref[...] = _dot(a_ref[...], w_ref[...]) + b_ref[...]


def expert_down(blk_expert, act, w2, b2):
    cap, d_ff = act.shape
    d = w2.shape[2]
    n_blk = cap // MOE_BLOCK
    grid_spec = pltpu.PrefetchScalarGridSpec(
        num_scalar_prefetch=1,
        grid=(n_blk,),
        in_specs=[pl.BlockSpec((MOE_BLOCK, d_ff), lambda b, be: (b, 0)),
                  pl.BlockSpec((None, d_ff, d), lambda b, be: (be[b], 0, 0)),
                  pl.BlockSpec((None, 1, d), lambda b, be: (be[b], 0, 0))],
        out_specs=pl.BlockSpec((MOE_BLOCK, d), lambda b, be: (b, 0)),
    )
    return pl.pallas_call(
        _expert_down_kernel,
        out_shape=jax.ShapeDtypeStruct((cap, d), F32),
        grid_spec=grid_spec,
        compiler_params=_params(1),
        name="expert_down",
    )(blk_expert, act, w2, b2)


def _combine_kernel(dest_ref, x_ref, gate_ref, nw_ref, yb_ref, o_ref, rows_ref, sem):
    tt = x_ref.shape[0]

    def row_copy(r, k, slot):
        return pltpu.make_async_copy(yb_ref.at[pl.ds(slot, 1), :],
                                     rows_ref.at[k, pl.ds(r, 1), :], sem)

    def start(r, carry):
        for k in range(TOP_K):
            row_copy(r, k, dest_ref[k, r]).start()
        return carry

    def wait(r, carry):
        for k in range(TOP_K):
            row_copy(r, k, dest_ref[k, r]).wait()
        return carry

    lax.fori_loop(0, tt, start, 0)
    lax.fori_loop(0, tt, wait, 0)
    gates = gate_ref[...]
    acc = x_ref[...]
    for k in range(TOP_K):
        acc = acc + gates[:, k:k + 1] * rows_ref[k]
    o_ref[...] = _rms(acc, nw_ref[...])


def moe_combine(dest, x, gates, norm_w, yb, tt):
    t, d = x.shape
    return pl.pallas_call(
        _combine_kernel,
        out_shape=jax.ShapeDtypeStruct((t, d), F32),
        grid=(t // tt,),
        in_specs=[pl.BlockSpec((TOP_K, tt), lambda i: (0, i), memory_space=pltpu.SMEM),
                  pl.BlockSpec((tt, d), lambda i: (i, 0)),
                  pl.BlockSpec((tt, TOP_K), lambda i: (i, 0)),
                  pl.BlockSpec((1, d), lambda i: (0, 0)),
                  pl.BlockSpec(memory_space=pl.ANY)],
        out_specs=pl.BlockSpec((tt, d), lambda i: (i, 0)),
        scratch_shapes=[pltpu.VMEM((TOP_K, tt, d), F32), pltpu.SemaphoreType.DMA(())],
        compiler_params=_params(1),
        name="moe_combine",
    )(dest, x, gates, norm_w, yb)


def _layer(x, mem_b, pos_b, norm_mix_w, w_in, conv_w, conv_b, dt_bias, a_log, d_skip, ssd_norm_w,
           w_ssd_branch, w_moba_branch, w_out, norm_cross_w, norm_mem_w, w_xq, w_xkv, w_xo,
           norm_ffn_w, w_router, b_router, w_mlp1, b_mlp1, w_mlp2, b_mlp2):
    t, d = x.shape
    n_ssd_heads = dt_bias.shape[0]
    d_inner = n_ssd_heads * SSD_HEAD_DIM
    bc_width = 2 * SSD_N_GROUPS * SSD_D_STATE
    conv_dim = d_inner + bc_width
    att_width = w_moba_branch.shape[0]
    n_att_heads = att_width // ATT_HEAD_DIM
    heads_per_group = n_ssd_heads // SSD_N_GROUPS
    n_exp = w_router.shape[1]

    o_z, o_xbc = 0, d_inner
    o_dt = o_xbc + conv_dim
    o_q = o_dt + n_ssd_heads
    o_k, o_v = o_q + att_width, o_q + 2 * att_width
    o_g = o_q + 3 * att_width
    w_dt = w_in[:, o_dt:o_q].reshape(d, SSD_N_GROUPS, heads_per_group)
    w_dt = jnp.pad(w_dt, ((0, 0), (0, 0), (0, LANES - heads_per_group)))
    w_dt = w_dt.reshape(d, SSD_N_GROUPS * LANES)
    w_cat = jnp.concatenate([w_in[:, o_z:o_dt], w_in[:, o_q:], w_dt], axis=1).astype(BF16)
    c_xbc = d_inner
    c_q = c_xbc + conv_dim
    c_k, c_v = c_q + att_width, c_q + 2 * att_width
    c_g = c_q + 3 * att_width
    c_dt = c_g + 2 * d
    proj = norm_matmul(x, norm_mix_w.reshape(1, d), w_cat, tm=1024, tn=512, out_dtype=F32)

    xbc_act = conv_silu(proj, c_xbc, conv_dim, conv_w, conv_b.reshape(1, conv_dim), tt=512, tc=1024)

    def per_group(v):
        v = v.reshape(SSD_N_GROUPS, 1, heads_per_group)
        return jnp.pad(v, ((0, 0), (0, 0), (0, LANES - heads_per_group)))

    y_ssd = ssd_scan(xbc_act, proj, c_dt // LANES, per_group(dt_bias), per_group(a_log),
                     jnp.repeat(d_skip, SSD_HEAD_DIM).reshape(1, d_inner),
                     ssd_norm_w.reshape(1, d_inner), d_inner)

    half = ATT_HEAD_DIM // 2
    inv_freq = ROPE_THETA ** (-jnp.arange(half, dtype=F32) / half)
    freq2 = jnp.concatenate([inv_freq, inv_freq]).reshape(1, ATT_HEAD_DIM)
    sign2 = jnp.concatenate([-jnp.ones((half,), F32), jnp.ones((half,), F32)]).reshape(1, ATT_HEAD_DIM)
    cos2, sin2 = rope_tables(pos_b.astype(F32).reshape(t, 1), freq2, sign2, tt=2048)
    q_rot, k_rot, v_t, k_mean = moba_prep(proj, c_q // ATT_HEAD_DIM, c_k // ATT_HEAD_DIM,
                                          c_v // ATT_HEAD_DIM, n_att_heads, cos2, sin2, tt=2048)
    y_moba = moba_attention(q_rot, k_rot, v_t, k_mean)

    mixed = gated_merge(y_ssd, y_moba, w_ssd_branch.astype(BF16), w_moba_branch.astype(BF16),
                        proj, c_g, tm=512, tn=512)
    x1 = matmul_residual(mixed, w_out.astype(BF16), x, tm=1024, tn=512)

    kv = norm_matmul(mem_b, norm_mem_w.reshape(1, d), w_xkv.astype(BF16),
                     tm=mem_b.shape[0], tn=512, out_dtype=BF16)
    x2 = cross_attention(x1, norm_cross_w.reshape(1, d), w_xq.astype(BF16), kv,
                         w_xo.astype(BF16), tm=256)

    w_rt = w_router.T
    w_rt_hi = w_rt.astype(BF16)
    w_rt_lo = (w_rt - w_rt_hi.astype(F32)).astype(BF16)
    hf, idx_t, gate_t, rank_t, counts = moe_router(x2, norm_ffn_w.reshape(1, d), w_rt_hi, w_rt_lo,
                                                   b_router.reshape(n_exp, 1), tm=512)
    counts = counts[:, 0]
    padded = (counts + MOE_BLOCK - 1) // MOE_BLOCK * MOE_BLOCK
    pad_end = jnp.cumsum(padded)
    pad_start = pad_end - padded
    dest = pad_start[idx_t] + rank_t
    n_blk = -(-(t * TOP_K) // MOE_BLOCK) + n_exp
    cap = n_blk * MOE_BLOCK
    blk_expert = jnp.minimum(
        jnp.searchsorted(pad_end, jnp.arange(n_blk, dtype=I32) * MOE_BLOCK, side='right'),
        n_exp - 1).astype(I32)
    xb = moe_dispatch(dest, hf, jnp.zeros((cap, d), F32), tt=256)
    act = expert_up(blk_expert, xb, w_mlp1.astype(BF16), b_mlp1.reshape(n_exp, 1, -1), tf=1024)
    yb = expert_down(blk_expert, act, w_mlp2.astype(BF16), b_mlp2.reshape(n_exp, 1, d))
    return dict(proj=proj, xbc_act=xbc_act, y_ssd=y_ssd, y_moba=y_moba, mixed=mixed, x1=x1, x2=x2,
                hf=hf, idx_t=idx_t, gate_t=gate_t, dest=dest, yb=yb)


def kernel(x, mem, positions, norm_mix_w, w_in, conv_w, conv_b, dt_bias, a_log, d_skip, ssd_norm_w,
           w_ssd_branch, w_moba_branch, w_out, norm_cross_w, norm_mem_w, w_xq, w_xkv, w_xo,
           norm_ffn_w, w_router, b_router, w_mlp1, b_mlp1, w_mlp2, b_mlp2, norm_final_w):
    assert w_in.shape[0] == 1, "depth-1 trunk only"
    d = x.shape[-1]
    outs = []
    for b in range(x.shape[0]):
        s = _layer(x[b], mem[b], positions[b], norm_mix_w[0], w_in[0], conv_w[0], conv_b[0],
                   dt_bias[0], a_log[0], d_skip[0], ssd_norm_w[0], w_ssd_branch[0],
                   w_moba_branch[0], w_out[0], norm_cross_w[0], norm_mem_w[0], w_xq[0], w_xkv[0],
                   w_xo[0], norm_ffn_w[0], w_router[0], b_router[0], w_mlp1[0], b_mlp1[0],
                   w_mlp2[0], b_mlp2[0])
        outs.append(moe_combine(s["dest"], s["x2"], s["gate_t"].T, norm_final_w.reshape(1, d),
                                s["yb"], tt=128))
    return jnp.stack(outs, axis=0)
```

```python
import functools

import jax
import jax.numpy as jnp
from jax import lax
from jax.experimental import pallas as pl
from jax.experimental.pallas import tpu as pltpu

F32 = jnp.float32
BF16 = jnp.bfloat16
I32 = jnp.int32

EPS = 1e-5
SSD_HEAD_DIM = 64
SSD_N_GROUPS = 8
SSD_D_STATE = 128
SSD_CONV_WIDTH = 4
SSD_CHUNK = 256
ATT_HEAD_DIM = 128
MOBA_BLOCK = 256
MOBA_TOPK = 3
ROPE_THETA = 10000.0
XATT_N_HEADS = 4
TOP_K = 4
SWIGLU_ALPHA = 1.702
SWIGLU_LIMIT = 7.0
MOE_BLOCK = 256

LANES = 128
VMEM_LIMIT = 56 * 1024 * 1024
MASKED = -1e30

NT_DIMS = (((1,), (1,)), ((), ()))


def _params(n_axes):
    return pltpu.CompilerParams(dimension_semantics=("arbitrary",) * n_axes,
                                vmem_limit_bytes=VMEM_LIMIT)


def _dot(a, b):
    return jnp.dot(a, b, preferred_element_type=F32)


def _dot_nt(a, b):
    return lax.dot_general(a, b, NT_DIMS, preferred_element_type=F32)


def _split3(v):
    hi = v.astype(BF16)
    r = v - hi.astype(F32)
    mid = r.astype(BF16)
    lo = (r - mid.astype(F32)).astype(BF16)
    return hi, mid, lo


def _dot3_rhs_exact(a, b_exact):
    hi, mid, lo = _split3(a)
    return _dot(hi, b_exact) + _dot(mid, b_exact) + _dot(lo, b_exact)


def _dot3_lhs_exact(a_exact, b):
    hi, mid, lo = _split3(b)
    return _dot(a_exact, hi) + _dot(a_exact, mid) + _dot(a_exact, lo)


def _rms(x, g):
    return x * lax.rsqrt(jnp.mean(x * x, axis=-1, keepdims=True) + EPS) * g


def _sigmoid(x):
    return 1.0 / (1.0 + jnp.exp(-x))


def _norm_matmul_kernel(x_ref, g_ref, w_ref, o_ref, h_ref):
    @pl.when(pl.program_id(1) == 0)
    def _():
        h_ref[...] = _rms(x_ref[...], g_ref[...]).astype(BF16)

    o_ref[...] = _dot(h_ref[...], w_ref[...]).astype(o_ref.dtype)


def norm_matmul(x, g, w, tm, tn, out_dtype):
    m, k = x.shape
    n = w.shape[1]
    return pl.pallas_call(
        _norm_matmul_kernel,
        out_shape=jax.ShapeDtypeStruct((m, n), out_dtype),
        grid=(m // tm, n // tn),
        in_specs=[pl.BlockSpec((tm, k), lambda i, j: (i, 0)),
                  pl.BlockSpec((1, k), lambda i, j: (0, 0)),
                  pl.BlockSpec((k, tn), lambda i, j: (0, j))],
        out_specs=pl.BlockSpec((tm, tn), lambda i, j: (i, j)),
        scratch_shapes=[pltpu.VMEM((tm, k), BF16)],
        compiler_params=_params(2),
        name="norm_matmul",
    )(x, g, w)


def _rope_table_kernel(pos_ref, freq_ref, sign_ref, cos_ref, sin_ref):
    ang = pos_ref[...] * freq_ref[...]
    cos_ref[...] = jnp.cos(ang)
    sin_ref[...] = jnp.sin(ang) * sign_ref[...]


def rope_tables(pos_col, freq2, sign2, tt):
    t = pos_col.shape[0]
    d = freq2.shape[1]
    out = jax.ShapeDtypeStruct((t, d), F32)
    return pl.pallas_call(
        _rope_table_kernel,
        out_shape=(out, out),
        grid=(t // tt,),
        in_specs=[pl.BlockSpec((tt, 1), lambda i: (i, 0)),
                  pl.BlockSpec((1, d), lambda i: (0, 0)),
                  pl.BlockSpec((1, d), lambda i: (0, 0))],
        out_specs=(pl.BlockSpec((tt, d), lambda i: (i, 0)),
                   pl.BlockSpec((tt, d), lambda i: (i, 0))),
        compiler_params=_params(1),
        name="rope_tables",
    )(pos_col, freq2, sign2)


def _conv_silu_kernel(u_ref, w_ref, b_ref, o_ref, ext_ref, *, tt):
    halo = 8

    @pl.when(pl.program_id(1) == 0)
    def _():
        ext_ref[0:halo, :] = jnp.zeros((halo, ext_ref.shape[1]), F32)

    ext_ref[halo:halo + tt, :] = u_ref[...]
    acc = b_ref[...]
    for i in range(SSD_CONV_WIDTH):
        off = halo - (SSD_CONV_WIDTH - 1) + i
        acc = acc + w_ref[i:i + 1, :] * ext_ref[off:off + tt, :]
    o_ref[...] = acc * _sigmoid(acc)
    ext_ref[0:halo, :] = ext_ref[tt:tt + halo, :]


def conv_silu(proj, col_off, width, conv_w, conv_b, tt, tc):
    t = proj.shape[0]
    off_blk = col_off // tc
    return pl.pallas_call(
        functools.partial(_conv_silu_kernel, tt=tt),
        out_shape=jax.ShapeDtypeStruct((t, width), F32),
        grid=(width // tc, t // tt),
        in_specs=[pl.BlockSpec((tt, tc), lambda j, i: (i, off_blk + j)),
                  pl.BlockSpec((SSD_CONV_WIDTH, tc), lambda j, i: (0, j)),
                  pl.BlockSpec((1, tc), lambda j, i: (0, j))],
        out_specs=pl.BlockSpec((tt, tc), lambda j, i: (i, j)),
        scratch_shapes=[pltpu.VMEM((tt + 8, tc), F32)],
        compiler_params=_params(2),
        name="conv_silu",
    )(proj, conv_w, conv_b)


def _ssd_kernel(x_ref, b_ref, c_ref, dt_ref, z_ref, bias_ref, alog_ref, dskip_ref, nw_ref,
                o_ref, state_ref, *, heads_per_group):
    chunk = x_ref.shape[0]
    gw = x_ref.shape[1]
    g = pl.program_id(1)

    @pl.when(pl.program_id(0) == 0)
    def _():
        state_ref[g] = jnp.zeros(state_ref.shape[1:], F32)

    raw = dt_ref[...] + bias_ref[...]
    dt = jnp.maximum(raw, 0.0) + jnp.log1p(jnp.exp(-jnp.abs(raw)))
    a = -jnp.exp(alog_ref[...])
    da = dt * a
    row = lax.broadcasted_iota(I32, (chunk, chunk), 0)
    col = lax.broadcasted_iota(I32, (chunk, chunk), 1)
    causal = row >= col
    tril = jnp.where(causal, 1.0, 0.0).astype(BF16)
    acum = _dot3_lhs_exact(tril, da)
    acum_t = acum.T
    a_last = acum[chunk - 1:chunk, :]
    exp_acum = jnp.exp(acum)
    to_end = jnp.exp(a_last - acum) * dt
    exp_last = jnp.broadcast_to(jnp.exp(a_last), (8, LANES))

    hsel = lax.broadcasted_iota(I32, (LANES, gw), 0)
    lane_head = lax.broadcasted_iota(I32, (LANES, gw), 1) // SSD_HEAD_DIM
    expand = jnp.where(hsel == lane_head, 1.0, 0.0).astype(BF16)
    stacked = jnp.concatenate([dt, to_end, exp_acum, exp_last], axis=0)
    spread = _dot3_rhs_exact(stacked, expand)
    dt_x = spread[0:chunk]
    to_end_x = spread[chunk:2 * chunk]
    exp_acum_x = spread[2 * chunk:3 * chunk]
    exp_last_x = spread[3 * chunk:3 * chunk + 1]

    x = x_ref[...]
    bmat = b_ref[...]
    cb16 = c_ref[...].astype(BF16)
    bb16 = bmat.astype(BF16)
    cb = _dot_nt(cb16, bb16)
    xdt = (x * dt_x).astype(BF16)
    state_t = state_ref[g]
    y_off = _dot(cb16, state_t.astype(BF16)) * exp_acum_x

    pieces = []
    for j in range(heads_per_group):
        seg = acum[:, j:j + 1] - acum_t[j:j + 1, :]
        decay = jnp.where(causal, jnp.exp(seg), 0.0)
        w = (cb * decay).astype(BF16)
        pieces.append(_dot(w, xdt[:, j * SSD_HEAD_DIM:(j + 1) * SSD_HEAD_DIM]))
    y_diag = jnp.concatenate(pieces, axis=1)

    bt16 = bmat.T.astype(BF16)
    state_ref[g] = state_t * exp_last_x + _dot(bt16, (x * to_end_x).astype(BF16))

    y = y_diag + y_off + dskip_ref[...] * x
    z = z_ref[...]
    u = y * (z * _sigmoid(z))
    u = u * lax.rsqrt(jnp.mean(u * u, axis=-1, keepdims=True) + EPS)
    o_ref[...] = (u * nw_ref[...]).astype(o_ref.dtype)


def ssd_scan(xbc_act, proj, dt_col_blk, bias_g, alog_g, dskip_x, norm_w, d_inner):
    t = xbc_act.shape[0]
    gw = d_inner // SSD_N_GROUPS
    heads_per_group = gw // SSD_HEAD_DIM
    b_blk = d_inner // SSD_D_STATE
    c_blk = b_blk + SSD_N_GROUPS
    return pl.pallas_call(
        functools.partial(_ssd_kernel, heads_per_group=heads_per_group),
        out_shape=jax.ShapeDtypeStruct((t, d_inner), BF16),
        grid=(t // SSD_CHUNK, SSD_N_GROUPS),
        in_specs=[pl.BlockSpec((SSD_CHUNK, gw), lambda c, g: (c, g)),
                  pl.BlockSpec((SSD_CHUNK, SSD_D_STATE), lambda c, g: (c, b_blk + g)),
                  pl.BlockSpec((SSD_CHUNK, SSD_D_STATE), lambda c, g: (c, c_blk + g)),
                  pl.BlockSpec((SSD_CHUNK, LANES), lambda c, g: (c, dt_col_blk + g)),
                  pl.BlockSpec((SSD_CHUNK, gw), lambda c, g: (c, g)),
                  pl.BlockSpec((None, 1, LANES), lambda c, g: (g, 0, 0)),
                  pl.BlockSpec((None, 1, LANES), lambda c, g: (g, 0, 0)),
                  pl.BlockSpec((1, gw), lambda c, g: (0, g)),
                  pl.BlockSpec((1, gw), lambda c, g: (0, g))],
        out_specs=pl.BlockSpec((SSD_CHUNK, gw), lambda c, g: (c, g)),
        scratch_shapes=[pltpu.VMEM((SSD_N_GROUPS, SSD_D_STATE, gw), F32)],
        compiler_params=_params(2),
        name="ssd_scan",
    )(xbc_act, xbc_act, xbc_act, proj, proj, bias_g, alog_g, dskip_x, norm_w)


V_EXT_ROWS = ATT_HEAD_DIM + 16


def _moba_prep_kernel(q_ref, k_ref, v_ref, cos_ref, sin_ref, qo_ref, ke_ref, ve_ref, km_ref):
    cos = cos_ref[...]
    sin = sin_ref[...]
    half = ATT_HEAD_DIM // 2
    q = q_ref[...]
    k = k_ref[...]
    qr = q * cos + pltpu.roll(q, half, 1) * sin
    kr = k * cos + pltpu.roll(k, half, 1) * sin
    qo_ref[...] = qr.astype(BF16)
    n_blk = ke_ref.shape[0]
    first_blk = pl.program_id(1) * n_blk
    kr3 = kr.reshape(n_blk, MOBA_BLOCK, ATT_HEAD_DIM)
    km_ref[...] = jnp.mean(kr3, axis=1)
    lane = lax.broadcasted_iota(I32, (MOBA_BLOCK, LANES), 1)
    pad_row = lax.broadcasted_iota(I32, (V_EXT_ROWS - ATT_HEAD_DIM, MOBA_BLOCK), 0)
    ones_rows = jnp.where(pad_row == 0, 1.0, 0.0).astype(BF16)
    for b in range(n_blk):
        tag = jnp.where(lane == first_blk + b, 1.0, 0.0).astype(BF16)
        ke_ref[b] = jnp.concatenate([kr3[b].astype(BF16), tag], axis=1)
        v_t = v_ref[b * MOBA_BLOCK:(b + 1) * MOBA_BLOCK, :].T.astype(BF16)
        ve_ref[b] = jnp.concatenate([v_t, ones_rows], axis=0)


def moba_prep(proj, q_blk, k_blk, v_blk, n_heads, cos2, sin2, tt):
    t = proj.shape[0]
    nb = t // MOBA_BLOCK
    assert nb <= LANES, "block tags live in one 128-lane tile"
    bps = tt // MOBA_BLOCK
    d = ATT_HEAD_DIM
    return pl.pallas_call(
        _moba_prep_kernel,
        out_shape=(jax.ShapeDtypeStruct((n_heads, t, d), BF16),
                   jax.ShapeDtypeStruct((n_heads, nb, MOBA_BLOCK, d + LANES), BF16),
                   jax.ShapeDtypeStruct((n_heads, nb, V_EXT_ROWS, MOBA_BLOCK), BF16),
                   jax.ShapeDtypeStruct((n_heads, nb, d), F32)),
        grid=(n_heads, t // tt),
        in_specs=[pl.BlockSpec((tt, d), lambda h, i: (i, q_blk + h)),
                  pl.BlockSpec((tt, d), lambda h, i: (i, k_blk + h)),
                  pl.BlockSpec((tt, d), lambda h, i: (i, v_blk + h)),
                  pl.BlockSpec((tt, d), lambda h, i: (i, 0)),
                  pl.BlockSpec((tt, d), lambda h, i: (i, 0))],
        out_specs=(pl.BlockSpec((None, tt, d), lambda h, i: (h, i, 0)),
                   pl.BlockSpec((None, bps, MOBA_BLOCK, d + LANES), lambda h, i: (h, i, 0, 0)),
                   pl.BlockSpec((None, bps, V_EXT_ROWS, MOBA_BLOCK), lambda h, i: (h, i, 0, 0)),
                   pl.BlockSpec((None, bps, d), lambda h, i: (h, i, 0))),
        compiler_params=_params(2),
        name="moba_prep",
    )(proj, proj, proj, cos2, sin2)


M_INIT = -3.0e38
LOG2_E = 1.4426950408889634


def _moba_attn_kernel(q_ref, ke_ref, ve_ref, km_ref, o_ref, qe_ref, acc_ref, sa_ref, sb_ref, *,
                      blocks_per_tile):
    i = pl.program_id(1)
    nb = ke_ref.shape[0]
    qt = q_ref.shape[0]
    first = i * blocks_per_tile
    q = q_ref[...]

    km = km_ref[...].astype(BF16)
    if nb < LANES:
        km = jnp.concatenate([km, jnp.zeros((LANES - nb, ATT_HEAD_DIM), BF16)], axis=0)
    gate = _dot_nt(km, q)
    blk = lax.broadcasted_iota(I32, gate.shape, 0)
    own = first + lax.broadcasted_iota(I32, gate.shape, 1) // MOBA_BLOCK
    past = blk < own
    g = jnp.where(past, gate, -jnp.inf)
    chosen = jnp.zeros(gate.shape, F32)
    for _ in range(MOBA_TOPK):
        best = jnp.max(g, axis=0, keepdims=True)
        first_best = jnp.min(jnp.where(g == best, blk, LANES), axis=0, keepdims=True)
        pick = blk == first_best
        chosen = jnp.where(pick, 1.0, chosen)
        g = jnp.where(pick, -jnp.inf, g)
    visible = jnp.where(past, chosen, jnp.where(blk == own, 1.0, 0.0))
    tags = jnp.where(visible > 0.5, 0.0, MASKED).T
    qe_ref[...] = jnp.concatenate([q, tags.astype(BF16)], axis=1)

    c = (ATT_HEAD_DIM ** -0.5) * LOG2_E
    key_pos = lax.broadcasted_iota(I32, (MOBA_BLOCK, qt), 0)
    qry_pos = lax.broadcasted_iota(I32, (MOBA_BLOCK, qt), 1)
    acc_ref[...] = jnp.zeros(acc_ref.shape, F32)

    def qk(j):
        return _dot_nt(ke_ref[j], qe_ref[...])

    def softmax_pv(s, j, m, diag_offset):
        if diag_offset is not None:
            s = jnp.where(key_pos + diag_offset * MOBA_BLOCK <= qry_pos, s, MASKED)
        m_new = jnp.maximum(m, jnp.max(s, axis=0, keepdims=True))
        alpha = jnp.exp2((m - m_new) * c)
        p = jnp.exp2((s - m_new) * c).astype(BF16)
        acc_ref[...] = alpha * acc_ref[...] + _dot(ve_ref[j], p)
        return m_new

    sa_ref[...] = qk(0)

    def pair(jj, m):
        j = 2 * jj
        sb_ref[...] = qk(j + 1)
        m = softmax_pv(sa_ref[...], j, m, None)
        sa_ref[...] = qk(j + 2)
        return softmax_pv(sb_ref[...], j + 1, m, None)

    m = jnp.full((1, qt), M_INIT, F32)
    m = lax.fori_loop(0, first // 2, pair, m)
    bufs = (sa_ref, sb_ref)
    for dd in range(blocks_per_tile):
        if dd + 1 < blocks_per_tile:
            bufs[(dd + 1) % 2][...] = qk(first + dd + 1)
        m = softmax_pv(bufs[dd % 2][...], first + dd, m, dd)
    acc = acc_ref[...]
    out_t = acc[0:ATT_HEAD_DIM] / acc[ATT_HEAD_DIM:ATT_HEAD_DIM + 1]
    o_ref[...] = out_t.T.astype(o_ref.dtype)


def moba_attention(q_rot, k_ext, v_ext, k_mean, qt):
    n_heads, t, d = q_rot.shape
    nb = t // MOBA_BLOCK
    assert (qt // MOBA_BLOCK) % 2 == 0, "past blocks are consumed in pairs"
    return pl.pallas_call(
        functools.partial(_moba_attn_kernel, blocks_per_tile=qt // MOBA_BLOCK),
        out_shape=jax.ShapeDtypeStruct((t, n_heads * d), BF16),
        grid=(n_heads, t // qt),
        in_specs=[pl.BlockSpec((None, qt, d), lambda h, i: (h, i, 0)),
                  pl.BlockSpec((None, nb, MOBA_BLOCK, d + LANES), lambda h, i: (h, 0, 0, 0)),
                  pl.BlockSpec((None, nb, V_EXT_ROWS, MOBA_BLOCK), lambda h, i: (h, 0, 0, 0)),
                  pl.BlockSpec((None, nb, d), lambda h, i: (h, 0, 0))],
        out_specs=pl.BlockSpec((qt, d), lambda h, i: (i, h)),
        scratch_shapes=[pltpu.VMEM((qt, d + LANES), BF16),
                        pltpu.VMEM((V_EXT_ROWS, qt), F32),
                        pltpu.VMEM((MOBA_BLOCK, qt), F32),
                        pltpu.VMEM((MOBA_BLOCK, qt), F32)],
        compiler_params=_params(2),
        name="moba_attention",
    )(q_rot, k_ext, v_ext, k_mean)


def _gated_merge_kernel(ys_ref, ym_ref, ws_ref, wm_ref, gs_ref, gm_ref, o_ref):
    a = _dot(ys_ref[...], ws_ref[...])
    b = _dot(ym_ref[...], wm_ref[...])
    o_ref[...] = (_sigmoid(gs_ref[...]) * a + _sigmoid(gm_ref[...]) * b).astype(o_ref.dtype)


def gated_merge(y_ssd, y_moba, w_s, w_m, proj, gate_col_off, tm, tn):
    t = y_ssd.shape[0]
    n = w_s.shape[1]
    gs_blk = gate_col_off // tn
    gm_blk = gs_blk + n // tn
    return pl.pallas_call(
        _gated_merge_kernel,
        out_shape=jax.ShapeDtypeStruct((t, n), BF16),
        grid=(t // tm, n // tn),
        in_specs=[pl.BlockSpec((tm, y_ssd.shape[1]), lambda i, j: (i, 0)),
                  pl.BlockSpec((tm, y_moba.shape[1]), lambda i, j: (i, 0)),
                  pl.BlockSpec((w_s.shape[0], tn), lambda i, j: (0, j)),
                  pl.BlockSpec((w_m.shape[0], tn), lambda i, j: (0, j)),
                  pl.BlockSpec((tm, tn), lambda i, j: (i, gs_blk + j)),
                  pl.BlockSpec((tm, tn), lambda i, j: (i, gm_blk + j))],
        out_specs=pl.BlockSpec((tm, tn), lambda i, j: (i, j)),
        compiler_params=_params(2),
        name="gated_merge",
    )(y_ssd, y_moba, w_s, w_m, proj, proj)


def _matmul_residual_kernel(a_ref, w_ref, r_ref, o_ref):
    o_ref[...] = r_ref[...] + _dot(a_ref[...], w_ref[...])


def matmul_residual(a, w, res, tm, tn):
    m, k = a.shape
    n = w.shape[1]
    return pl.pallas_call(
        _matmul_residual_kernel,
        out_shape=jax.ShapeDtypeStruct((m, n), F32),
        grid=(m // tm, n // tn),
        in_specs=[pl.BlockSpec((tm, k), lambda i, j: (i, 0)),
                  pl.BlockSpec((k, tn), lambda i, j: (0, j)),
                  pl.BlockSpec((tm, tn), lambda i, j: (i, j))],
        out_specs=pl.BlockSpec((tm, tn), lambda i, j: (i, j)),
        compiler_params=_params(2),
        name="matmul_residual",
    )(a, w, res)


def _cross_attn_kernel(x_ref, g_ref, wq_ref, kv_ref, wo_ref, o_ref):
    x = x_ref[...]
    d = x.shape[1]
    hd = d // XATT_N_HEADS
    h = _rms(x, g_ref[...]).astype(BF16)
    q = _dot(h, wq_ref[...]).astype(BF16)
    outs = []
    for hh in range(XATT_N_HEADS):
        qh = q[:, hh * hd:(hh + 1) * hd]
        kh = kv_ref[:, hh * hd:(hh + 1) * hd]
        vh = kv_ref[:, d + hh * hd:d + (hh + 1) * hd]
        s = _dot_nt(qh, kh) * (hd ** -0.5)
        p = jnp.exp(s - jnp.max(s, axis=-1, keepdims=True))
        p = p / jnp.sum(p, axis=-1, keepdims=True)
        outs.append(_dot(p.astype(BF16), vh))
    o = jnp.concatenate(outs, axis=1).astype(BF16)
    o_ref[...] = x + _dot(o, wo_ref[...])


def cross_attention(x, g, w_q, kv, w_o, tm):
    t, d = x.shape
    const = lambda i: (0, 0)
    single = pl.Buffered(1)
    return pl.pallas_call(
        _cross_attn_kernel,
        out_shape=jax.ShapeDtypeStruct((t, d), F32),
        grid=(t // tm,),
        in_specs=[pl.BlockSpec((tm, d), lambda i: (i, 0)),
                  pl.BlockSpec((1, d), const),
                  pl.BlockSpec(w_q.shape, const, pipeline_mode=single),
                  pl.BlockSpec(kv.shape, const, pipeline_mode=single),
                  pl.BlockSpec(w_o.shape, const, pipeline_mode=single)],
        out_specs=pl.BlockSpec((tm, d), lambda i: (i, 0)),
        compiler_params=_params(1),
        name="cross_attention",
    )(x, g, w_q, kv, w_o)


def _router_kernel(x_ref, g_ref, whi_ref, wlo_ref, b_ref, h_ref, idx_ref, gate_ref, rank_ref,
                   cnt_ref, run_ref):
    tm = x_ref.shape[0]
    n_exp = whi_ref.shape[0]

    @pl.when(pl.program_id(0) == 0)
    def _():
        run_ref[...] = jnp.zeros(run_ref.shape, F32)

    h = _rms(x_ref[...], g_ref[...])
    h_ref[...] = h
    hi = h.astype(BF16)
    lo = (h - hi.astype(F32)).astype(BF16)
    logits = (_dot_nt(whi_ref[...], hi) + _dot_nt(whi_ref[...], lo) + _dot_nt(wlo_ref[...], hi)
              + b_ref[...])

    eid = lax.broadcasted_iota(I32, logits.shape, 0)
    g = logits
    member = jnp.zeros(logits.shape, F32)
    idxs, vals, picks = [], [], []
    for _ in range(TOP_K):
        best = jnp.max(g, axis=0, keepdims=True)
        first = jnp.min(jnp.where(g == best, eid, n_exp), axis=0, keepdims=True)
        pick = eid == first
        member = jnp.where(pick, 1.0, member)
        g = jnp.where(pick, -jnp.inf, g)
        idxs.append(first)
        vals.append(best)
        picks.append(pick)
    exps = [jnp.exp(v - vals[0]) for v in vals]
    denom = exps[0]
    for e in exps[1:]:
        denom = denom + e

    src = lax.broadcasted_iota(I32, (tm, tm), 0)
    dst = lax.broadcasted_iota(I32, (tm, tm), 1)
    before = jnp.where(src < dst, 1.0, 0.0).astype(BF16)
    run = run_ref[...]
    rank_all = _dot(member.astype(BF16), before) + run[:, 0:1]
    ranks = [jnp.sum(jnp.where(p, rank_all, 0.0), axis=0, keepdims=True) for p in picks]

    idx_ref[...] = jnp.concatenate(idxs, axis=0)
    gate_ref[...] = jnp.concatenate([e / denom for e in exps], axis=0)
    rank_ref[...] = jnp.concatenate(ranks, axis=0).astype(I32)
    run = run + jnp.sum(member, axis=1, keepdims=True)
    run_ref[...] = run
    cnt_ref[...] = run.astype(I32)


def moe_router(x, g, w_hi, w_lo, b_col, tm):
    t, d = x.shape
    n_exp = w_hi.shape[0]
    const = lambda i: (0, 0)
    return pl.pallas_call(
        _router_kernel,
        out_shape=(jax.ShapeDtypeStruct((t, d), F32),
                   jax.ShapeDtypeStruct((TOP_K, t), I32),
                   jax.ShapeDtypeStruct((TOP_K, t), F32),
                   jax.ShapeDtypeStruct((TOP_K, t), I32),
                   jax.ShapeDtypeStruct((n_exp, LANES), I32)),
        grid=(t // tm,),
        in_specs=[pl.BlockSpec((tm, d), lambda i: (i, 0)),
                  pl.BlockSpec((1, d), const),
                  pl.BlockSpec((n_exp, d), const),
                  pl.BlockSpec((n_exp, d), const),
                  pl.BlockSpec((n_exp, 1), const)],
        out_specs=(pl.BlockSpec((tm, d), lambda i: (i, 0)),
                   pl.BlockSpec((TOP_K, tm), lambda i: (0, i)),
                   pl.BlockSpec((TOP_K, tm), lambda i: (0, i)),
                   pl.BlockSpec((TOP_K, tm), lambda i: (0, i)),
                   pl.BlockSpec((n_exp, LANES), const)),
        scratch_shapes=[pltpu.VMEM((n_exp, LANES), F32)],
        compiler_params=_params(1),
        name="moe_router",
    )(x, g, w_hi, w_lo, b_col)


def _dispatch_kernel(dest_ref, h_ref, buf_in_ref, buf_ref, sem):
    del buf_in_ref
    tt = h_ref.shape[0]

    def row_copy(r, slot):
        return pltpu.make_async_copy(h_ref.at[pl.ds(r, 1), :], buf_ref.at[pl.ds(slot, 1), :], sem)

    def start(r, carry):
        for k in range(TOP_K):
            row_copy(r, dest_ref[k, r]).start(priority=k % 2)
        return carry

    def wait(r, carry):
        for k in range(TOP_K):
            row_copy(r, dest_ref[k, r]).wait()
        return carry

    lax.fori_loop(0, tt, start, 0)
    lax.fori_loop(0, tt, wait, 0)


def moe_dispatch(dest, h, buf_zero, tt):
    t, d = h.shape
    return pl.pallas_call(
        _dispatch_kernel,
        out_shape=jax.ShapeDtypeStruct(buf_zero.shape, buf_zero.dtype),
        grid=(t // tt,),
        in_specs=[pl.BlockSpec((TOP_K, tt), lambda i: (0, i), memory_space=pltpu.SMEM),
                  pl.BlockSpec((tt, d), lambda i: (i, 0)),
                  pl.BlockSpec(memory_space=pl.ANY)],
        out_specs=pl.BlockSpec(memory_space=pl.ANY),
        scratch_shapes=[pltpu.SemaphoreType.DMA(())],
        input_output_aliases={2: 0},
        compiler_params=_params(1),
        name="moe_dispatch",
    )(dest, h, buf_zero)


def _run_start(b, be_ref):
    return (b == 0) | (be_ref[b] != be_ref[jnp.maximum(b - 1, 0)])


def _expert_up_kernel(be_ref, nx_ref, nu_ref, x_ref, w_hbm, bg_ref, bl_ref, o_ref,
                      wbuf, w16, sem, slot_ref, *, tf, n_f):
    f = pl.program_id(0)
    b = pl.program_id(1)
    e = be_ref[b]

    def fetch(expert, slot):
        copies = []
        for half in range(2):
            col = pl.multiple_of((half * n_f + f) * tf, tf)
            copies.append(pltpu.make_async_copy(w_hbm.at[expert, :, pl.ds(col, tf)],
                                                wbuf.at[slot, half], sem.at[slot, half]))
        return copies

    @pl.when(b < nu_ref[0])
    def _():
        @pl.when(b == 0)
        def _():
            slot_ref[0] = 0
            for cp in fetch(e, 0):
                cp.start()

        @pl.when(_run_start(b, be_ref))
        def _():
            slot = slot_ref[0]
            for cp in fetch(e, slot):
                cp.wait()
            nxt = nx_ref[b]

            @pl.when(nxt >= 0)
            def _():
                for cp in fetch(nxt, 1 - slot):
                    cp.start()

            for half in range(2):
                w16[half] = wbuf[slot, half].astype(BF16)
            slot_ref[0] = 1 - slot

        x = x_ref[...].astype(BF16)
        glu = _dot(x, w16[0]) + bg_ref[...]
        lin = _dot(x, w16[1]) + bl_ref[...]
        glu = jnp.minimum(glu, SWIGLU_LIMIT)
        lin = jnp.clip(lin, -SWIGLU_LIMIT, SWIGLU_LIMIT)
        o_ref[...] = (glu * _sigmoid(SWIGLU_ALPHA * glu) * (lin + 1.0)).astype(o_ref.dtype)

    @pl.when(b >= nu_ref[0])
    def _():
        o_ref[...] = jnp.zeros(o_ref.shape, o_ref.dtype)


def expert_up(blk_expert, nxt_expert, n_used, xb, w1, b1, tf):
    cap, d = xb.shape
    d_ff = w1.shape[2] // 2
    n_f = d_ff // tf
    n_blk = cap // MOE_BLOCK

    def blk(b, nu):
        return jnp.minimum(b, nu[0] - 1)

    grid_spec = pltpu.PrefetchScalarGridSpec(
        num_scalar_prefetch=3,
        grid=(n_f, n_blk),
        in_specs=[pl.BlockSpec((MOE_BLOCK, d), lambda f, b, be, nx, nu: (blk(b, nu), 0)),
                  pl.BlockSpec(memory_space=pl.ANY),
                  pl.BlockSpec((None, 1, tf), lambda f, b, be, nx, nu: (be[blk(b, nu)], 0, f)),
                  pl.BlockSpec((None, 1, tf),
                               lambda f, b, be, nx, nu: (be[blk(b, nu)], 0, n_f + f))],
        out_specs=pl.BlockSpec((MOE_BLOCK, tf), lambda f, b, be, nx, nu: (b, f)),
        scratch_shapes=[pltpu.VMEM((2, 2, d, tf), F32),
                        pltpu.VMEM((2, d, tf), BF16),
                        pltpu.SemaphoreType.DMA((2, 2)),
                        pltpu.SMEM((1,), I32)],
    )
    return pl.pallas_call(
        functools.partial(_expert_up_kernel, tf=tf, n_f=n_f),
        out_shape=jax.ShapeDtypeStruct((cap, d_ff), BF16),
        grid_spec=grid_spec,
        compiler_params=_params(2),
        name="expert_up",
    )(blk_expert, nxt_expert, n_used, xb, w1, b1, b1)


def _expert_down_kernel(be_ref, nx_ref, nu_ref, a_ref, w_hbm, b_ref, o_ref, wbuf, w16, sem, slot_ref):
    b = pl.program_id(0)
    e = be_ref[b]

    def fetch(expert, slot):
        return pltpu.make_async_copy(w_hbm.at[expert], wbuf.at[slot], sem.at[slot])

    @pl.when(b < nu_ref[0])
    def _():
        @pl.when(b == 0)
        def _():
            slot_ref[0] = 0
            fetch(e, 0).start()

        @pl.when(_run_start(b, be_ref))
        def _():
            slot = slot_ref[0]
            fetch(e, slot).wait()
            nxt = nx_ref[b]

            @pl.when(nxt >= 0)
            def _():
                fetch(nxt, 1 - slot).start()

            w16[...] = wbuf[slot].astype(BF16)
            slot_ref[0] = 1 - slot

        o_ref[...] = _dot(a_ref[...], w16[...]) + b_ref[...]

    @pl.when(b >= nu_ref[0])
    def _():
        o_ref[...] = jnp.zeros(o_ref.shape, o_ref.dtype)


def expert_down(blk_expert, nxt_expert, n_used, act, w2, b2):
    cap, d_ff = act.shape
    d = w2.shape[2]
    n_blk = cap // MOE_BLOCK

    def blk(b, nu):
        return jnp.minimum(b, nu[0] - 1)

    grid_spec = pltpu.PrefetchScalarGridSpec(
        num_scalar_prefetch=3,
        grid=(n_blk,),
        in_specs=[pl.BlockSpec((MOE_BLOCK, d_ff), lambda b, be, nx, nu: (blk(b, nu), 0)),
                  pl.BlockSpec(memory_space=pl.ANY),
                  pl.BlockSpec((None, 1, d), lambda b, be, nx, nu: (be[blk(b, nu)], 0, 0))],
        out_specs=pl.BlockSpec((MOE_BLOCK, d), lambda b, be, nx, nu: (b, 0)),
        scratch_shapes=[pltpu.VMEM((2, d_ff, d), F32),
                        pltpu.VMEM((d_ff, d), BF16),
                        pltpu.SemaphoreType.DMA((2,)),
                        pltpu.SMEM((1,), I32)],
    )
    return pl.pallas_call(
        _expert_down_kernel,
        out_shape=jax.ShapeDtypeStruct((cap, d), F32),
        grid_spec=grid_spec,
        compiler_params=_params(1),
        name="expert_down",
    )(blk_expert, nxt_expert, n_used, act, w2, b2)


def _combine_kernel(dest_ref, dest_next_ref, x_ref, gate_ref, nw_ref, yb_ref, o_ref, rows_ref, sem):
    i = pl.program_id(0)
    tt = x_ref.shape[0]
    slot = i % 2

    def row_copy(d_ref, r, k, s):
        return pltpu.make_async_copy(yb_ref.at[pl.ds(d_ref[k, r], 1), :],
                                     rows_ref.at[s, k, pl.ds(r, 1), :], sem.at[s])

    def start_tile(d_ref, s):
        def body(r, carry):
            for k in range(TOP_K):
                row_copy(d_ref, r, k, s).start(priority=k % 2)
            return carry
        lax.fori_loop(0, tt, body, 0)

    @pl.when(i == 0)
    def _():
        start_tile(dest_ref, 0)

    @pl.when(i + 1 < pl.num_programs(0))
    def _():
        start_tile(dest_next_ref, 1 - slot)

    def wait(r, carry):
        for k in range(TOP_K):
            row_copy(dest_ref, r, k, slot).wait()
        return carry

    lax.fori_loop(0, tt, wait, 0)
    gates = gate_ref[...]
    acc = x_ref[...]
    for k in range(TOP_K):
        acc = acc + gates[:, k:k + 1] * rows_ref[slot, k]
    o_ref[...] = _rms(acc, nw_ref[...])


def moe_combine(dest, x, gates, norm_w, yb, tt):
    t, d = x.shape
    last = t // tt - 1
    return pl.pallas_call(
        _combine_kernel,
        out_shape=jax.ShapeDtypeStruct((t, d), F32),
        grid=(t // tt,),
        in_specs=[pl.BlockSpec((TOP_K, tt), lambda i: (0, i), memory_space=pltpu.SMEM),
                  pl.BlockSpec((TOP_K, tt), lambda i: (0, jnp.minimum(i + 1, last)),
                               memory_space=pltpu.SMEM),
                  pl.BlockSpec((tt, d), lambda i: (i, 0)),
                  pl.BlockSpec((tt, TOP_K), lambda i: (i, 0)),
                  pl.BlockSpec((1, d), lambda i: (0, 0)),
                  pl.BlockSpec(memory_space=pl.ANY)],
        out_specs=pl.BlockSpec((tt, d), lambda i: (i, 0)),
        scratch_shapes=[pltpu.VMEM((2, TOP_K, tt, d), F32), pltpu.SemaphoreType.DMA((2,))],
        compiler_params=_params(1),
        name="moe_combine",
    )(dest, dest, x, gates, norm_w, yb)


def _lookup(table, idx):
    hit = idx[..., None] == jnp.arange(table.shape[0], dtype=I32)
    return jnp.sum(jnp.where(hit, table, 0), axis=-1)


def _layer(x, mem_b, pos_b, norm_mix_w, w_in, conv_w, conv_b, dt_bias, a_log, d_skip, ssd_norm_w,
           w_ssd_branch, w_moba_branch, w_out, norm_cross_w, norm_mem_w, w_xq, w_xkv, w_xo,
           norm_ffn_w, w_router, b_router, w_mlp1, b_mlp1, w_mlp2, b_mlp2):
    t, d = x.shape
    n_ssd_heads = dt_bias.shape[0]
    d_inner = n_ssd_heads * SSD_HEAD_DIM
    bc_width = 2 * SSD_N_GROUPS * SSD_D_STATE
    conv_dim = d_inner + bc_width
    att_width = w_moba_branch.shape[0]
    n_att_heads = att_width // ATT_HEAD_DIM
    heads_per_group = n_ssd_heads // SSD_N_GROUPS
    n_exp = w_router.shape[1]

    o_z, o_xbc = 0, d_inner
    o_dt = o_xbc + conv_dim
    o_q = o_dt + n_ssd_heads
    o_k, o_v = o_q + att_width, o_q + 2 * att_width
    o_g = o_q + 3 * att_width
    w_dt = w_in[:, o_dt:o_q].reshape(d, SSD_N_GROUPS, heads_per_group)
    w_dt = jnp.pad(w_dt, ((0, 0), (0, 0), (0, LANES - heads_per_group)))
    w_dt = w_dt.reshape(d, SSD_N_GROUPS * LANES)
    w_cat = jnp.concatenate([w_in[:, o_z:o_dt], w_in[:, o_q:], w_dt], axis=1).astype(BF16)
    c_xbc = d_inner
    c_q = c_xbc + conv_dim
    c_k, c_v = c_q + att_width, c_q + 2 * att_width
    c_g = c_q + 3 * att_width
    c_dt = c_g + 2 * d
    proj = norm_matmul(x, norm_mix_w.reshape(1, d), w_cat, tm=1024, tn=1024, out_dtype=F32)

    xbc_act = conv_silu(proj, c_xbc, conv_dim, conv_w, conv_b.reshape(1, conv_dim), tt=512, tc=1024)

    def per_group(v):
        v = v.reshape(SSD_N_GROUPS, 1, heads_per_group)
        return jnp.pad(v, ((0, 0), (0, 0), (0, LANES - heads_per_group)))

    y_ssd = ssd_scan(xbc_act, proj, c_dt // LANES, per_group(dt_bias), per_group(a_log),
                     jnp.repeat(d_skip, SSD_HEAD_DIM).reshape(1, d_inner),
                     ssd_norm_w.reshape(1, d_inner), d_inner)

    half = ATT_HEAD_DIM // 2
    inv_freq = ROPE_THETA ** (-jnp.arange(half, dtype=F32) / half)
    freq2 = jnp.concatenate([inv_freq, inv_freq]).reshape(1, ATT_HEAD_DIM)
    sign2 = jnp.concatenate([-jnp.ones((half,), F32), jnp.ones((half,), F32)]).reshape(1, ATT_HEAD_DIM)
    cos2, sin2 = rope_tables(pos_b.astype(F32).reshape(t, 1), freq2, sign2, tt=2048)
    q_rot, k_ext, v_ext, k_mean = moba_prep(proj, c_q // ATT_HEAD_DIM, c_k // ATT_HEAD_DIM,
                                            c_v // ATT_HEAD_DIM, n_att_heads, cos2, sin2, tt=2048)
    y_moba = moba_attention(q_rot, k_ext, v_ext, k_mean, qt=1024)

    mixed = gated_merge(y_ssd, y_moba, w_ssd_branch.astype(BF16), w_moba_branch.astype(BF16),
                        proj, c_g, tm=512, tn=512)
    x1 = matmul_residual(mixed, w_out.astype(BF16), x, tm=1024, tn=512)

    kv = norm_matmul(mem_b, norm_mem_w.reshape(1, d), w_xkv.astype(BF16),
                     tm=mem_b.shape[0], tn=512, out_dtype=BF16)
    x2 = cross_attention(x1, norm_cross_w.reshape(1, d), w_xq.astype(BF16), kv,
                         w_xo.astype(BF16), tm=256)

    w_rt = w_router.T
    w_rt_hi = w_rt.astype(BF16)
    w_rt_lo = (w_rt - w_rt_hi.astype(F32)).astype(BF16)
    hf, idx_t, gate_t, rank_t, counts = moe_router(x2, norm_ffn_w.reshape(1, d), w_rt_hi, w_rt_lo,
                                                   b_router.reshape(n_exp, 1), tm=512)
    counts = counts[:, 0]
    padded = (counts + MOE_BLOCK - 1) // MOE_BLOCK * MOE_BLOCK
    pad_end = jnp.cumsum(padded)
    pad_start = pad_end - padded
    dest = _lookup(pad_start, idx_t) + rank_t
    n_blk = -(-(t * TOP_K) // MOE_BLOCK) + n_exp
    cap = n_blk * MOE_BLOCK
    blk_row = jnp.arange(n_blk, dtype=I32) * MOE_BLOCK
    blk_expert = jnp.minimum(
        jnp.sum((pad_end[None, :] <= blk_row[:, None]).astype(I32), axis=1), n_exp - 1)
    n_used = pad_end[-1] // MOE_BLOCK
    run_end = _lookup(pad_end, blk_expert) // MOE_BLOCK
    nxt_expert = jnp.where(run_end < n_used,
                           _lookup(blk_expert, jnp.minimum(run_end, n_blk - 1)), -1)
    n_used = n_used.reshape(1)
    xb = moe_dispatch(dest, hf, jnp.zeros((cap, d), F32), tt=256)
    act = expert_up(blk_expert, nxt_expert, n_used, xb, w_mlp1, b_mlp1.reshape(n_exp, 1, -1),
                    tf=1024)
    yb = expert_down(blk_expert, nxt_expert, n_used, act, w_mlp2, b_mlp2.reshape(n_exp, 1, d))
    return dict(proj=proj, xbc_act=xbc_act, y_ssd=y_ssd, y_moba=y_moba, mixed=mixed, x1=x1, x2=x2,
                hf=hf, idx_t=idx_t, gate_t=gate_t, dest=dest, yb=yb)


def kernel(x, mem, positions, norm_mix_w, w_in, conv_w, conv_b, dt_bias, a_log, d_skip, ssd_norm_w,
           w_ssd_branch, w_moba_branch, w_out, norm_cross_w, norm_mem_w, w_xq, w_xkv, w_xo,
           norm_ffn_w, w_router, b_router, w_mlp1, b_mlp1, w_mlp2, b_mlp2, norm_final_w):
    assert w_in.shape[0] == 1, "depth-1 trunk only"
    d = x.shape[-1]
    outs = []
    for b in range(x.shape[0]):
        s = _layer(x[b], mem[b], positions[b], norm_mix_w[0], w_in[0], conv_w[0], conv_b[0],
                   dt_bias[0], a_log[0], d_skip[0], ssd_norm_w[0], w_ssd_branch[0],
                   w_moba_branch[0], w_out[0], norm_cross_w[0], norm_mem_w[0], w_xq[0], w_xkv[0],
                   w_xo[0], norm_ffn_w[0], w_router[0], b_router[0], w_mlp1[0], b_mlp1[0],
                   w_mlp2[0], b_mlp2[0])
        outs.append(moe_combine(s["dest"], s["x2"], s["gate_t"].T, norm_final_w.reshape(1, d),
                                s["yb"], tt=128))
    return jnp.stack(outs, axis=0)
```

```python
import functools

import jax
import jax.numpy as jnp
from jax import lax
from jax.experimental import pallas as pl
from jax.experimental.pallas import tpu as pltpu

F32 = jnp.float32
BF16 = jnp.bfloat16
I32 = jnp.int32

EPS = 1e-5
SSD_HEAD_DIM = 64
SSD_N_GROUPS = 8
SSD_D_STATE = 128
SSD_CONV_WIDTH = 4
SSD_CHUNK = 256
ATT_HEAD_DIM = 128
MOBA_BLOCK = 256
MOBA_TOPK = 3
ROPE_THETA = 10000.0
XATT_N_HEADS = 4
TOP_K = 4
SWIGLU_ALPHA = 1.702
SWIGLU_LIMIT = 7.0
MOE_BLOCK = 256

LANES = 128
VMEM_LIMIT = 56 * 1024 * 1024
MASKED = -1e30

NT_DIMS = (((1,), (1,)), ((), ()))


def _params(n_axes):
    return pltpu.CompilerParams(dimension_semantics=("arbitrary",) * n_axes,
                                vmem_limit_bytes=VMEM_LIMIT)


def _dot(a, b):
    return jnp.dot(a, b, preferred_element_type=F32)


def _dot_nt(a, b):
    return lax.dot_general(a, b, NT_DIMS, preferred_element_type=F32)


def _split3(v):
    hi = v.astype(BF16)
    r = v - hi.astype(F32)
    mid = r.astype(BF16)
    lo = (r - mid.astype(F32)).astype(BF16)
    return hi, mid, lo


def _dot3_rhs_exact(a, b_exact):
    hi, mid, lo = _split3(a)
    return _dot(hi, b_exact) + _dot(mid, b_exact) + _dot(lo, b_exact)


def _dot3_lhs_exact(a_exact, b):
    hi, mid, lo = _split3(b)
    return _dot(a_exact, hi) + _dot(a_exact, mid) + _dot(a_exact, lo)


def _rms(x, g):
    return x * lax.rsqrt(jnp.mean(x * x, axis=-1, keepdims=True) + EPS) * g


def _sigmoid(x):
    return 1.0 / (1.0 + jnp.exp(-x))


def _norm_matmul_kernel(x_ref, g_ref, w_ref, o_ref, h_ref):
    @pl.when(pl.program_id(1) == 0)
    def _():
        h_ref[...] = _rms(x_ref[...], g_ref[...]).astype(BF16)

    o_ref[...] = _dot(h_ref[...], w_ref[...]).astype(o_ref.dtype)


def norm_matmul(x, g, w, tm, tn, out_dtype):
    m, k = x.shape
    n = w.shape[1]
    return pl.pallas_call(
        _norm_matmul_kernel,
        out_shape=jax.ShapeDtypeStruct((m, n), out_dtype),
        grid=(m // tm, n // tn),
        in_specs=[pl.BlockSpec((tm, k), lambda i, j: (i, 0)),
                  pl.BlockSpec((1, k), lambda i, j: (0, 0)),
                  pl.BlockSpec((k, tn), lambda i, j: (0, j))],
        out_specs=pl.BlockSpec((tm, tn), lambda i, j: (i, j)),
        scratch_shapes=[pltpu.VMEM((tm, k), BF16)],
        compiler_params=_params(2),
        name="norm_matmul",
    )(x, g, w)


def _rope_table_kernel(pos_ref, freq_ref, sign_ref, cos_ref, sin_ref):
    ang = pos_ref[...] * freq_ref[...]
    cos_ref[...] = jnp.cos(ang)
    sin_ref[...] = jnp.sin(ang) * sign_ref[...]


def rope_tables(pos_col, freq2, sign2, tt):
    t = pos_col.shape[0]
    d = freq2.shape[1]
    out = jax.ShapeDtypeStruct((t, d), F32)
    return pl.pallas_call(
        _rope_table_kernel,
        out_shape=(out, out),
        grid=(t // tt,),
        in_specs=[pl.BlockSpec((tt, 1), lambda i: (i, 0)),
                  pl.BlockSpec((1, d), lambda i: (0, 0)),
                  pl.BlockSpec((1, d), lambda i: (0, 0))],
        out_specs=(pl.BlockSpec((tt, d), lambda i: (i, 0)),
                   pl.BlockSpec((tt, d), lambda i: (i, 0))),
        compiler_params=_params(1),
        name="rope_tables",
    )(pos_col, freq2, sign2)


def _conv_silu_kernel(u_ref, w_ref, b_ref, o_ref, ext_ref, *, tt):
    halo = 8

    @pl.when(pl.program_id(1) == 0)
    def _():
        ext_ref[0:halo, :] = jnp.zeros((halo, ext_ref.shape[1]), F32)

    ext_ref[halo:halo + tt, :] = u_ref[...]
    acc = b_ref[...]
    for i in range(SSD_CONV_WIDTH):
        off = halo - (SSD_CONV_WIDTH - 1) + i
        acc = acc + w_ref[i:i + 1, :] * ext_ref[off:off + tt, :]
    o_ref[...] = acc * _sigmoid(acc)
    ext_ref[0:halo, :] = ext_ref[tt:tt + halo, :]


def conv_silu(proj, col_off, width, conv_w, conv_b, tt, tc):
    t = proj.shape[0]
    off_blk = col_off // tc
    return pl.pallas_call(
        functools.partial(_conv_silu_kernel, tt=tt),
        out_shape=jax.ShapeDtypeStruct((t, width), F32),
        grid=(width // tc, t // tt),
        in_specs=[pl.BlockSpec((tt, tc), lambda j, i: (i, off_blk + j)),
                  pl.BlockSpec((SSD_CONV_WIDTH, tc), lambda j, i: (0, j)),
                  pl.BlockSpec((1, tc), lambda j, i: (0, j))],
        out_specs=pl.BlockSpec((tt, tc), lambda j, i: (i, j)),
        scratch_shapes=[pltpu.VMEM((tt + 8, tc), F32)],
        compiler_params=_params(2),
        name="conv_silu",
    )(proj, conv_w, conv_b)


def _ssd_kernel(x_ref, b_ref, c_ref, dt_ref, z_ref, bias_ref, alog_ref, dskip_ref, nw_ref,
                o_ref, state_ref, *, heads_per_group):
    chunk = x_ref.shape[0]
    gw = x_ref.shape[1]
    g = pl.program_id(1)

    @pl.when(pl.program_id(0) == 0)
    def _():
        state_ref[g] = jnp.zeros(state_ref.shape[1:], F32)

    raw = dt_ref[...] + bias_ref[...]
    dt = jnp.maximum(raw, 0.0) + jnp.log1p(jnp.exp(-jnp.abs(raw)))
    a = -jnp.exp(alog_ref[...])
    da = dt * a
    row = lax.broadcasted_iota(I32, (chunk, chunk), 0)
    col = lax.broadcasted_iota(I32, (chunk, chunk), 1)
    causal = row >= col
    tril = jnp.where(causal, 1.0, 0.0).astype(BF16)
    acum = _dot3_lhs_exact(tril, da)
    acum_t = acum.T
    a_last = acum[chunk - 1:chunk, :]
    exp_acum = jnp.exp(acum)
    to_end = jnp.exp(a_last - acum) * dt
    exp_last = jnp.broadcast_to(jnp.exp(a_last), (8, LANES))

    hsel = lax.broadcasted_iota(I32, (LANES, gw), 0)
    lane_head = lax.broadcasted_iota(I32, (LANES, gw), 1) // SSD_HEAD_DIM
    expand = jnp.where(hsel == lane_head, 1.0, 0.0).astype(BF16)
    stacked = jnp.concatenate([dt, to_end, exp_acum, exp_last], axis=0)
    spread = _dot3_rhs_exact(stacked, expand)
    dt_x = spread[0:chunk]
    to_end_x = spread[chunk:2 * chunk]
    exp_acum_x = spread[2 * chunk:3 * chunk]
    exp_last_x = spread[3 * chunk:3 * chunk + 1]

    x = x_ref[...]
    bmat = b_ref[...]
    cb16 = c_ref[...].astype(BF16)
    bb16 = bmat.astype(BF16)
    cb = _dot_nt(cb16, bb16)
    xdt = (x * dt_x).astype(BF16)
    state_t = state_ref[g]
    y_off = _dot(cb16, state_t.astype(BF16)) * exp_acum_x

    pieces = []
    for j in range(heads_per_group):
        seg = acum[:, j:j + 1] - acum_t[j:j + 1, :]
        decay = jnp.where(causal, jnp.exp(seg), 0.0)
        w = (cb * decay).astype(BF16)
        pieces.append(_dot(w, xdt[:, j * SSD_HEAD_DIM:(j + 1) * SSD_HEAD_DIM]))
    y_diag = jnp.concatenate(pieces, axis=1)

    bt16 = bmat.T.astype(BF16)
    state_ref[g] = state_t * exp_last_x + _dot(bt16, (x * to_end_x).astype(BF16))

    y = y_diag + y_off + dskip_ref[...] * x
    z = z_ref[...]
    u = y * (z * _sigmoid(z))
    u = u * lax.rsqrt(jnp.mean(u * u, axis=-1, keepdims=True) + EPS)
    o_ref[...] = (u * nw_ref[...]).astype(o_ref.dtype)


def ssd_scan(xbc_act, proj, dt_col_blk, bias_g, alog_g, dskip_x, norm_w, d_inner):
    t = xbc_act.shape[0]
    gw = d_inner // SSD_N_GROUPS
    heads_per_group = gw // SSD_HEAD_DIM
    b_blk = d_inner // SSD_D_STATE
    c_blk = b_blk + SSD_N_GROUPS
    return pl.pallas_call(
        functools.partial(_ssd_kernel, heads_per_group=heads_per_group),
        out_shape=jax.ShapeDtypeStruct((t, d_inner), BF16),
        grid=(t // SSD_CHUNK, SSD_N_GROUPS),
        in_specs=[pl.BlockSpec((SSD_CHUNK, gw), lambda c, g: (c, g)),
                  pl.BlockSpec((SSD_CHUNK, SSD_D_STATE), lambda c, g: (c, b_blk + g)),
                  pl.BlockSpec((SSD_CHUNK, SSD_D_STATE), lambda c, g: (c, c_blk + g)),
                  pl.BlockSpec((SSD_CHUNK, LANES), lambda c, g: (c, dt_col_blk + g)),
                  pl.BlockSpec((SSD_CHUNK, gw), lambda c, g: (c, g)),
                  pl.BlockSpec((None, 1, LANES), lambda c, g: (g, 0, 0)),
                  pl.BlockSpec((None, 1, LANES), lambda c, g: (g, 0, 0)),
                  pl.BlockSpec((1, gw), lambda c, g: (0, g)),
                  pl.BlockSpec((1, gw), lambda c, g: (0, g))],
        out_specs=pl.BlockSpec((SSD_CHUNK, gw), lambda c, g: (c, g)),
        scratch_shapes=[pltpu.VMEM((SSD_N_GROUPS, SSD_D_STATE, gw), F32)],
        compiler_params=_params(2),
        name="ssd_scan",
    )(xbc_act, xbc_act, xbc_act, proj, proj, bias_g, alog_g, dskip_x, norm_w)


V_EXT_ROWS = ATT_HEAD_DIM + 16


def _moba_prep_kernel(q_ref, k_ref, v_ref, cos_ref, sin_ref, qo_ref, ke_ref, ve_ref, km_ref):
    cos = cos_ref[...]
    sin = sin_ref[...]
    half = ATT_HEAD_DIM // 2
    q = q_ref[...]
    k = k_ref[...]
    qr = q * cos + pltpu.roll(q, half, 1) * sin
    kr = k * cos + pltpu.roll(k, half, 1) * sin
    qo_ref[...] = qr.astype(BF16)
    n_blk = ke_ref.shape[0]
    first_blk = pl.program_id(1) * n_blk
    kr3 = kr.reshape(n_blk, MOBA_BLOCK, ATT_HEAD_DIM)
    km_ref[...] = jnp.mean(kr3, axis=1)
    lane = lax.broadcasted_iota(I32, (MOBA_BLOCK, LANES), 1)
    pad_row = lax.broadcasted_iota(I32, (V_EXT_ROWS - ATT_HEAD_DIM, MOBA_BLOCK), 0)
    ones_rows = jnp.where(pad_row == 0, 1.0, 0.0).astype(BF16)
    for b in range(n_blk):
        tag = jnp.where(lane == first_blk + b, 1.0, 0.0).astype(BF16)
        ke_ref[b] = jnp.concatenate([kr3[b].astype(BF16), tag], axis=1)
        v_t = v_ref[b * MOBA_BLOCK:(b + 1) * MOBA_BLOCK, :].T.astype(BF16)
        ve_ref[b] = jnp.concatenate([v_t, ones_rows], axis=0)


def moba_prep(proj, q_blk, k_blk, v_blk, n_heads, cos2, sin2, tt):
    t = proj.shape[0]
    nb = t // MOBA_BLOCK
    assert nb <= LANES, "block tags live in one 128-lane tile"
    bps = tt // MOBA_BLOCK
    d = ATT_HEAD_DIM
    return pl.pallas_call(
        _moba_prep_kernel,
        out_shape=(jax.ShapeDtypeStruct((n_heads, t, d), BF16),
                   jax.ShapeDtypeStruct((n_heads, nb, MOBA_BLOCK, d + LANES), BF16),
                   jax.ShapeDtypeStruct((n_heads, nb, V_EXT_ROWS, MOBA_BLOCK), BF16),
                   jax.ShapeDtypeStruct((n_heads, nb, d), F32)),
        grid=(n_heads, t // tt),
        in_specs=[pl.BlockSpec((tt, d), lambda h, i: (i, q_blk + h)),
                  pl.BlockSpec((tt, d), lambda h, i: (i, k_blk + h)),
                  pl.BlockSpec((tt, d), lambda h, i: (i, v_blk + h)),
                  pl.BlockSpec((tt, d), lambda h, i: (i, 0)),
                  pl.BlockSpec((tt, d), lambda h, i: (i, 0))],
        out_specs=(pl.BlockSpec((None, tt, d), lambda h, i: (h, i, 0)),
                   pl.BlockSpec((None, bps, MOBA_BLOCK, d + LANES), lambda h, i: (h, i, 0, 0)),
                   pl.BlockSpec((None, bps, V_EXT_ROWS, MOBA_BLOCK), lambda h, i: (h, i, 0, 0)),
                   pl.BlockSpec((None, bps, d), lambda h, i: (h, i, 0))),
        compiler_params=_params(2),
        name="moba_prep",
    )(proj, proj, proj, cos2, sin2)


M_INIT = -3.0e38
LOG2_E = 1.4426950408889634


def _moba_attn_kernel(q_ref, ke_ref, ve_ref, km_ref, o_ref, qe_ref, acc_ref, sa_ref, sb_ref, *,
                      blocks_per_tile):
    i = pl.program_id(1)
    nb = ke_ref.shape[0]
    qt = q_ref.shape[0]
    first = i * blocks_per_tile
    q = q_ref[...]

    km = km_ref[...].astype(BF16)
    if nb < LANES:
        km = jnp.concatenate([km, jnp.zeros((LANES - nb, ATT_HEAD_DIM), BF16)], axis=0)
    gate = _dot_nt(km, q)
    blk = lax.broadcasted_iota(I32, gate.shape, 0)
    own = first + lax.broadcasted_iota(I32, gate.shape, 1) // MOBA_BLOCK
    past = blk < own
    g = jnp.where(past, gate, -jnp.inf)
    chosen = jnp.zeros(gate.shape, F32)
    for _ in range(MOBA_TOPK):
        best = jnp.max(g, axis=0, keepdims=True)
        first_best = jnp.min(jnp.where(g == best, blk, LANES), axis=0, keepdims=True)
        pick = blk == first_best
        chosen = jnp.where(pick, 1.0, chosen)
        g = jnp.where(pick, -jnp.inf, g)
    visible = jnp.where(past, chosen, jnp.where(blk == own, 1.0, 0.0))
    tags = jnp.where(visible > 0.5, 0.0, MASKED).T
    qe_ref[...] = jnp.concatenate([q, tags.astype(BF16)], axis=1)

    c = (ATT_HEAD_DIM ** -0.5) * LOG2_E
    acc_ref[...] = jnp.zeros(acc_ref.shape, F32)

    def qk(j, lo):
        return _dot_nt(ke_ref[j], qe_ref[lo:, :])

    def softmax_pv(s, j, m, lo, causal):
        if causal:
            width = qt - lo
            key_pos = lax.broadcasted_iota(I32, (MOBA_BLOCK, width), 0)
            col_pos = lax.broadcasted_iota(I32, (MOBA_BLOCK, width), 1)
            s = jnp.where(key_pos <= col_pos, s, MASKED)
        m_new = jnp.maximum(m, jnp.max(s, axis=0, keepdims=True))
        alpha = jnp.exp2((m - m_new) * c)
        p = jnp.exp2((s - m_new) * c).astype(BF16)
        acc_ref[:, lo:] = alpha * acc_ref[:, lo:] + _dot(ve_ref[j], p)
        return m_new

    sa_ref[...] = qk(0, 0)

    def pair(jj, m):
        j = 2 * jj
        sb_ref[...] = qk(j + 1, 0)
        m = softmax_pv(sa_ref[...], j, m, 0, False)
        sa_ref[...] = qk(j + 2, 0)
        return softmax_pv(sb_ref[...], j + 1, m, 0, False)

    m = jnp.full((1, qt), M_INIT, F32)
    m = lax.fori_loop(0, first // 2, pair, m)
    bufs = (sa_ref, sb_ref)
    for dd in range(blocks_per_tile):
        lo = dd * MOBA_BLOCK
        if dd + 1 < blocks_per_tile:
            bufs[(dd + 1) % 2][:, lo + MOBA_BLOCK:] = qk(first + dd + 1, lo + MOBA_BLOCK)
        m_new = softmax_pv(bufs[dd % 2][:, lo:], first + dd, m[:, lo:], lo, True)
        m = m_new if lo == 0 else jnp.concatenate([m[:, :lo], m_new], axis=1)
    acc = acc_ref[...]
    out_t = acc[0:ATT_HEAD_DIM] / acc[ATT_HEAD_DIM:ATT_HEAD_DIM + 1]
    o_ref[...] = out_t.T.astype(o_ref.dtype)


def moba_attention(q_rot, k_ext, v_ext, k_mean, qt):
    n_heads, t, d = q_rot.shape
    nb = t // MOBA_BLOCK
    assert (qt // MOBA_BLOCK) % 2 == 0, "past blocks are consumed in pairs"
    return pl.pallas_call(
        functools.partial(_moba_attn_kernel, blocks_per_tile=qt // MOBA_BLOCK),
        out_shape=jax.ShapeDtypeStruct((t, n_heads * d), BF16),
        grid=(n_heads, t // qt),
        in_specs=[pl.BlockSpec((None, qt, d), lambda h, i: (h, i, 0)),
                  pl.BlockSpec((None, nb, MOBA_BLOCK, d + LANES), lambda h, i: (h, 0, 0, 0)),
                  pl.BlockSpec((None, nb, V_EXT_ROWS, MOBA_BLOCK), lambda h, i: (h, 0, 0, 0)),
                  pl.BlockSpec((None, nb, d), lambda h, i: (h, 0, 0))],
        out_specs=pl.BlockSpec((qt, d), lambda h, i: (i, h)),
        scratch_shapes=[pltpu.VMEM((qt, d + LANES), BF16),
                        pltpu.VMEM((V_EXT_ROWS, qt), F32),
                        pltpu.VMEM((MOBA_BLOCK, qt), F32),
                        pltpu.VMEM((MOBA_BLOCK, qt), F32)],
        compiler_params=_params(2),
        name="moba_attention",
    )(q_rot, k_ext, v_ext, k_mean)


def _gated_merge_kernel(ys_ref, ym_ref, ws_ref, wm_ref, gs_ref, gm_ref, o_ref):
    a = _dot(ys_ref[...], ws_ref[...])
    b = _dot(ym_ref[...], wm_ref[...])
    o_ref[...] = (_sigmoid(gs_ref[...]) * a + _sigmoid(gm_ref[...]) * b).astype(o_ref.dtype)


def gated_merge(y_ssd, y_moba, w_s, w_m, proj, gate_col_off, tm, tn):
    t = y_ssd.shape[0]
    n = w_s.shape[1]
    gs_blk = gate_col_off // tn
    gm_blk = gs_blk + n // tn
    return pl.pallas_call(
        _gated_merge_kernel,
        out_shape=jax.ShapeDtypeStruct((t, n), BF16),
        grid=(t // tm, n // tn),
        in_specs=[pl.BlockSpec((tm, y_ssd.shape[1]), lambda i, j: (i, 0)),
                  pl.BlockSpec((tm, y_moba.shape[1]), lambda i, j: (i, 0)),
                  pl.BlockSpec((w_s.shape[0], tn), lambda i, j: (0, j)),
                  pl.BlockSpec((w_m.shape[0], tn), lambda i, j: (0, j)),
                  pl.BlockSpec((tm, tn), lambda i, j: (i, gs_blk + j)),
                  pl.BlockSpec((tm, tn), lambda i, j: (i, gm_blk + j))],
        out_specs=pl.BlockSpec((tm, tn), lambda i, j: (i, j)),
        compiler_params=_params(2),
        name="gated_merge",
    )(y_ssd, y_moba, w_s, w_m, proj, proj)


def _matmul_residual_kernel(a_ref, w_ref, r_ref, o_ref):
    o_ref[...] = r_ref[...] + _dot(a_ref[...], w_ref[...])


def matmul_residual(a, w, res, tm, tn):
    m, k = a.shape
    n = w.shape[1]
    return pl.pallas_call(
        _matmul_residual_kernel,
        out_shape=jax.ShapeDtypeStruct((m, n), F32),
        grid=(m // tm, n // tn),
        in_specs=[pl.BlockSpec((tm, k), lambda i, j: (i, 0)),
                  pl.BlockSpec((k, tn), lambda i, j: (0, j)),
                  pl.BlockSpec((tm, tn), lambda i, j: (i, j))],
        out_specs=pl.BlockSpec((tm, tn), lambda i, j: (i, j)),
        compiler_params=_params(2),
        name="matmul_residual",
    )(a, w, res)


def _cross_attn_kernel(x_ref, g_ref, wq_ref, kv_ref, wo_ref, o_ref):
    x = x_ref[...]
    d = x.shape[1]
    hd = d // XATT_N_HEADS
    h = _rms(x, g_ref[...]).astype(BF16)
    q = _dot(h, wq_ref[...]).astype(BF16)
    outs = []
    for hh in range(XATT_N_HEADS):
        qh = q[:, hh * hd:(hh + 1) * hd]
        kh = kv_ref[:, hh * hd:(hh + 1) * hd]
        vh = kv_ref[:, d + hh * hd:d + (hh + 1) * hd]
        s = _dot_nt(qh, kh) * (hd ** -0.5)
        p = jnp.exp(s - jnp.max(s, axis=-1, keepdims=True))
        p = p / jnp.sum(p, axis=-1, keepdims=True)
        outs.append(_dot(p.astype(BF16), vh))
    o = jnp.concatenate(outs, axis=1).astype(BF16)
    o_ref[...] = x + _dot(o, wo_ref[...])


def cross_attention(x, g, w_q, kv, w_o, tm):
    t, d = x.shape
    const = lambda i: (0, 0)
    single = pl.Buffered(1)
    return pl.pallas_call(
        _cross_attn_kernel,
        out_shape=jax.ShapeDtypeStruct((t, d), F32),
        grid=(t // tm,),
        in_specs=[pl.BlockSpec((tm, d), lambda i: (i, 0)),
                  pl.BlockSpec((1, d), const),
                  pl.BlockSpec(w_q.shape, const, pipeline_mode=single),
                  pl.BlockSpec(kv.shape, const, pipeline_mode=single),
                  pl.BlockSpec(w_o.shape, const, pipeline_mode=single)],
        out_specs=pl.BlockSpec((tm, d), lambda i: (i, 0)),
        compiler_params=_params(1),
        name="cross_attention",
    )(x, g, w_q, kv, w_o)


def _router_kernel(x_ref, g_ref, whi_ref, wlo_ref, b_ref, h_ref, idx_ref, gate_ref, rank_ref,
                   cnt_ref, run_ref):
    tm = x_ref.shape[0]
    n_exp = whi_ref.shape[0]

    @pl.when(pl.program_id(0) == 0)
    def _():
        run_ref[...] = jnp.zeros(run_ref.shape, F32)

    h = _rms(x_ref[...], g_ref[...])
    h_ref[...] = h
    hi = h.astype(BF16)
    lo = (h - hi.astype(F32)).astype(BF16)
    logits = (_dot_nt(whi_ref[...], hi) + _dot_nt(whi_ref[...], lo) + _dot_nt(wlo_ref[...], hi)
              + b_ref[...])

    eid = lax.broadcasted_iota(I32, logits.shape, 0)
    g = logits
    member = jnp.zeros(logits.shape, F32)
    idxs, vals, picks = [], [], []
    for _ in range(TOP_K):
        best = jnp.max(g, axis=0, keepdims=True)
        first = jnp.min(jnp.where(g == best, eid, n_exp), axis=0, keepdims=True)
        pick = eid == first
        member = jnp.where(pick, 1.0, member)
        g = jnp.where(pick, -jnp.inf, g)
        idxs.append(first)
        vals.append(best)
        picks.append(pick)
    exps = [jnp.exp(v - vals[0]) for v in vals]
    denom = exps[0]
    for e in exps[1:]:
        denom = denom + e

    src = lax.broadcasted_iota(I32, (tm, tm), 0)
    dst = lax.broadcasted_iota(I32, (tm, tm), 1)
    before = jnp.where(src < dst, 1.0, 0.0).astype(BF16)
    run = run_ref[...]
    rank_all = _dot(member.astype(BF16), before) + run[:, 0:1]
    ranks = [jnp.sum(jnp.where(p, rank_all, 0.0), axis=0, keepdims=True) for p in picks]

    idx_ref[...] = jnp.concatenate(idxs, axis=0)
    gate_ref[...] = jnp.concatenate([e / denom for e in exps], axis=0)
    rank_ref[...] = jnp.concatenate(ranks, axis=0).astype(I32)
    run = run + jnp.sum(member, axis=1, keepdims=True)
    run_ref[...] = run
    cnt_ref[...] = run.astype(I32)


def moe_router(x, g, w_hi, w_lo, b_col, tm):
    t, d = x.shape
    n_exp = w_hi.shape[0]
    const = lambda i: (0, 0)
    return pl.pallas_call(
        _router_kernel,
        out_shape=(jax.ShapeDtypeStruct((t, d), F32),
                   jax.ShapeDtypeStruct((TOP_K, t), I32),
                   jax.ShapeDtypeStruct((TOP_K, t), F32),
                   jax.ShapeDtypeStruct((TOP_K, t), I32),
                   jax.ShapeDtypeStruct((n_exp, LANES), I32)),
        grid=(t // tm,),
        in_specs=[pl.BlockSpec((tm, d), lambda i: (i, 0)),
                  pl.BlockSpec((1, d), const),
                  pl.BlockSpec((n_exp, d), const),
                  pl.BlockSpec((n_exp, d), const),
                  pl.BlockSpec((n_exp, 1), const)],
        out_specs=(pl.BlockSpec((tm, d), lambda i: (i, 0)),
                   pl.BlockSpec((TOP_K, tm), lambda i: (0, i)),
                   pl.BlockSpec((TOP_K, tm), lambda i: (0, i)),
                   pl.BlockSpec((TOP_K, tm), lambda i: (0, i)),
                   pl.BlockSpec((n_exp, LANES), const)),
        scratch_shapes=[pltpu.VMEM((n_exp, LANES), F32)],
        compiler_params=_params(1),
        name="moe_router",
    )(x, g, w_hi, w_lo, b_col)


def _dispatch_kernel(pe_ref, nu_ref, dest_ref, h_ref, buf_ref, zero_ref, sem, zero_sem, *, n_blk):
    tt = h_ref.shape[0]
    n_exp = pe_ref.shape[0]

    @pl.when(pl.program_id(0) == 0)
    def _():
        zero_ref[...] = jnp.zeros(zero_ref.shape, zero_ref.dtype)

        def zero_block(row0):
            return pltpu.make_async_copy(zero_ref, buf_ref.at[pl.ds(row0, MOE_BLOCK), :], zero_sem)

        def last_block(e):
            return pl.multiple_of(jnp.maximum(pe_ref[e] - MOE_BLOCK, 0), MOE_BLOCK)

        for e in range(n_exp):
            zero_block(last_block(e)).start()

        def tail_start(b, carry):
            zero_block(pl.multiple_of(b * MOE_BLOCK, MOE_BLOCK)).start()
            return carry

        def tail_wait(b, carry):
            zero_block(pl.multiple_of(b * MOE_BLOCK, MOE_BLOCK)).wait()
            return carry

        lax.fori_loop(nu_ref[0], n_blk, tail_start, 0)
        for e in range(n_exp):
            zero_block(last_block(e)).wait()
        lax.fori_loop(nu_ref[0], n_blk, tail_wait, 0)

    def row_copy(r, slot):
        return pltpu.make_async_copy(h_ref.at[pl.ds(r, 1), :], buf_ref.at[pl.ds(slot, 1), :], sem)

    def start(r, carry):
        for k in range(TOP_K):
            row_copy(r, dest_ref[k, r]).start(priority=k % 2)
        return carry

    def wait(r, carry):
        for k in range(TOP_K):
            row_copy(r, dest_ref[k, r]).wait()
        return carry

    lax.fori_loop(0, tt, start, 0)
    lax.fori_loop(0, tt, wait, 0)


def moe_dispatch(pad_end, n_used, dest, h, cap, tt):
    t, d = h.shape
    grid_spec = pltpu.PrefetchScalarGridSpec(
        num_scalar_prefetch=2,
        grid=(t // tt,),
        in_specs=[pl.BlockSpec((TOP_K, tt), lambda i, pe, nu: (0, i), memory_space=pltpu.SMEM),
                  pl.BlockSpec((tt, d), lambda i, pe, nu: (i, 0))],
        out_specs=pl.BlockSpec(memory_space=pl.ANY),
        scratch_shapes=[pltpu.VMEM((MOE_BLOCK, d), h.dtype),
                        pltpu.SemaphoreType.DMA(()),
                        pltpu.SemaphoreType.DMA(())],
    )
    return pl.pallas_call(
        functools.partial(_dispatch_kernel, n_blk=cap // MOE_BLOCK),
        out_shape=jax.ShapeDtypeStruct((cap, d), h.dtype),
        grid_spec=grid_spec,
        compiler_params=_params(1),
        name="moe_dispatch",
    )(pad_end, n_used, dest, h)


def _run_start(b, be_ref):
    return (b == 0) | (be_ref[b] != be_ref[jnp.maximum(b - 1, 0)])


def _expert_up_kernel(be_ref, nx_ref, nu_ref, x_ref, w_hbm, bg_ref, bl_ref, o_ref,
                      wbuf, w16, sem, slot_ref, *, tf, n_f):
    f = pl.program_id(0)
    b = pl.program_id(1)
    e = be_ref[b]

    def fetch(expert, slot):
        copies = []
        for half in range(2):
            col = pl.multiple_of((half * n_f + f) * tf, tf)
            copies.append(pltpu.make_async_copy(w_hbm.at[expert, :, pl.ds(col, tf)],
                                                wbuf.at[slot, half], sem.at[slot, half]))
        return copies

    @pl.when(b < nu_ref[0])
    def _():
        @pl.when(b == 0)
        def _():
            slot_ref[0] = 0
            for cp in fetch(e, 0):
                cp.start()

        @pl.when(_run_start(b, be_ref))
        def _():
            slot = slot_ref[0]
            for cp in fetch(e, slot):
                cp.wait()
            nxt = nx_ref[b]

            @pl.when(nxt >= 0)
            def _():
                for cp in fetch(nxt, 1 - slot):
                    cp.start()

            for half in range(2):
                w16[half] = wbuf[slot, half].astype(BF16)
            slot_ref[0] = 1 - slot

        x = x_ref[...].astype(BF16)
        glu = _dot(x, w16[0]) + bg_ref[...]
        lin = _dot(x, w16[1]) + bl_ref[...]
        glu = jnp.minimum(glu, SWIGLU_LIMIT)
        lin = jnp.clip(lin, -SWIGLU_LIMIT, SWIGLU_LIMIT)
        o_ref[...] = (glu * _sigmoid(SWIGLU_ALPHA * glu) * (lin + 1.0)).astype(o_ref.dtype)

    @pl.when(b >= nu_ref[0])
    def _():
        o_ref[...] = jnp.zeros(o_ref.shape, o_ref.dtype)


def expert_up(blk_expert, nxt_expert, n_used, xb, w1, b1, tf):
    cap, d = xb.shape
    d_ff = w1.shape[2] // 2
    n_f = d_ff // tf
    n_blk = cap // MOE_BLOCK

    def blk(b, nu):
        return jnp.minimum(b, nu[0] - 1)

    grid_spec = pltpu.PrefetchScalarGridSpec(
        num_scalar_prefetch=3,
        grid=(n_f, n_blk),
        in_specs=[pl.BlockSpec((MOE_BLOCK, d), lambda f, b, be, nx, nu: (blk(b, nu), 0)),
                  pl.BlockSpec(memory_space=pl.ANY),
                  pl.BlockSpec((None, 1, tf), lambda f, b, be, nx, nu: (be[blk(b, nu)], 0, f)),
                  pl.BlockSpec((None, 1, tf),
                               lambda f, b, be, nx, nu: (be[blk(b, nu)], 0, n_f + f))],
        out_specs=pl.BlockSpec((MOE_BLOCK, tf), lambda f, b, be, nx, nu: (b, f)),
        scratch_shapes=[pltpu.VMEM((2, 2, d, tf), F32),
                        pltpu.VMEM((2, d, tf), BF16),
                        pltpu.SemaphoreType.DMA((2, 2)),
                        pltpu.SMEM((1,), I32)],
    )
    return pl.pallas_call(
        functools.partial(_expert_up_kernel, tf=tf, n_f=n_f),
        out_shape=jax.ShapeDtypeStruct((cap, d_ff), BF16),
        grid_spec=grid_spec,
        compiler_params=_params(2),
        name="expert_up",
    )(blk_expert, nxt_expert, n_used, xb, w1, b1, b1)


def _expert_down_kernel(be_ref, nx_ref, nu_ref, a_ref, w_hbm, b_ref, o_ref, wbuf, w16, sem, slot_ref):
    b = pl.program_id(0)
    e = be_ref[b]

    def fetch(expert, slot):
        return pltpu.make_async_copy(w_hbm.at[expert], wbuf.at[slot], sem.at[slot])

    @pl.when(b < nu_ref[0])
    def _():
        @pl.when(b == 0)
        def _():
            slot_ref[0] = 0
            fetch(e, 0).start()

        @pl.when(_run_start(b, be_ref))
        def _():
            slot = slot_ref[0]
            fetch(e, slot).wait()
            nxt = nx_ref[b]

            @pl.when(nxt >= 0)
            def _():
                fetch(nxt, 1 - slot).start()

            w16[...] = wbuf[slot].astype(BF16)
            slot_ref[0] = 1 - slot

        o_ref[...] = _dot(a_ref[...], w16[...]) + b_ref[...]

    @pl.when(b >= nu_ref[0])
    def _():
        o_ref[...] = jnp.zeros(o_ref.shape, o_ref.dtype)


def expert_down(blk_expert, nxt_expert, n_used, act, w2, b2):
    cap, d_ff = act.shape
    d = w2.shape[2]
    n_blk = cap // MOE_BLOCK

    def blk(b, nu):
        return jnp.minimum(b, nu[0] - 1)

    grid_spec = pltpu.PrefetchScalarGridSpec(
        num_scalar_prefetch=3,
        grid=(n_blk,),
        in_specs=[pl.BlockSpec((MOE_BLOCK, d_ff), lambda b, be, nx, nu: (blk(b, nu), 0)),
                  pl.BlockSpec(memory_space=pl.ANY),
                  pl.BlockSpec((None, 1, d), lambda b, be, nx, nu: (be[blk(b, nu)], 0, 0))],
        out_specs=pl.BlockSpec((MOE_BLOCK, d), lambda b, be, nx, nu: (b, 0)),
        scratch_shapes=[pltpu.VMEM((2, d_ff, d), F32),
                        pltpu.VMEM((d_ff, d), BF16),
                        pltpu.SemaphoreType.DMA((2,)),
                        pltpu.SMEM((1,), I32)],
    )
    return pl.pallas_call(
        _expert_down_kernel,
        out_shape=jax.ShapeDtypeStruct((cap, d), F32),
        grid_spec=grid_spec,
        compiler_params=_params(1),
        name="expert_down",
    )(blk_expert, nxt_expert, n_used, act, w2, b2)


def _combine_kernel(dest_ref, dest_next_ref, x_ref, gate_ref, nw_ref, yb_ref, o_ref, rows_ref, sem):
    i = pl.program_id(0)
    tt = x_ref.shape[0]
    slot = i % 2

    def row_copy(d_ref, r, k, s):
        return pltpu.make_async_copy(yb_ref.at[pl.ds(d_ref[k, r], 1), :],
                                     rows_ref.at[s, k, pl.ds(r, 1), :], sem.at[s])

    def start_tile(d_ref, s):
        def body(r, carry):
            for k in range(TOP_K):
                row_copy(d_ref, r, k, s).start(priority=k % 2)
            return carry
        lax.fori_loop(0, tt, body, 0)

    @pl.when(i == 0)
    def _():
        start_tile(dest_ref, 0)

    @pl.when(i + 1 < pl.num_programs(0))
    def _():
        start_tile(dest_next_ref, 1 - slot)

    def wait(r, carry):
        for k in range(TOP_K):
            row_copy(dest_ref, r, k, slot).wait()
        return carry

    lax.fori_loop(0, tt, wait, 0)
    gates = gate_ref[...]
    acc = x_ref[...]
    for k in range(TOP_K):
        acc = acc + gates[:, k:k + 1] * rows_ref[slot, k]
    o_ref[...] = _rms(acc, nw_ref[...])


def moe_combine(dest, x, gates, norm_w, yb, tt):
    t, d = x.shape
    last = t // tt - 1
    return pl.pallas_call(
        _combine_kernel,
        out_shape=jax.ShapeDtypeStruct((t, d), F32),
        grid=(t // tt,),
        in_specs=[pl.BlockSpec((TOP_K, tt), lambda i: (0, i), memory_space=pltpu.SMEM),
                  pl.BlockSpec((TOP_K, tt), lambda i: (0, jnp.minimum(i + 1, last)),
                               memory_space=pltpu.SMEM),
                  pl.BlockSpec((tt, d), lambda i: (i, 0)),
                  pl.BlockSpec((tt, TOP_K), lambda i: (i, 0)),
                  pl.BlockSpec((1, d), lambda i: (0, 0)),
                  pl.BlockSpec(memory_space=pl.ANY)],
        out_specs=pl.BlockSpec((tt, d), lambda i: (i, 0)),
        scratch_shapes=[pltpu.VMEM((2, TOP_K, tt, d), F32), pltpu.SemaphoreType.DMA((2,))],
        compiler_params=_params(1),
        name="moe_combine",
    )(dest, dest, x, gates, norm_w, yb)


def _lookup(table, idx):
    hit = idx[..., None] == jnp.arange(table.shape[0], dtype=I32)
    return jnp.sum(jnp.where(hit, table, 0), axis=-1)


def _layer(x, mem_b, pos_b, norm_mix_w, w_in, conv_w, conv_b, dt_bias, a_log, d_skip, ssd_norm_w,
           w_ssd_branch, w_moba_branch, w_out, norm_cross_w, norm_mem_w, w_xq, w_xkv, w_xo,
           norm_ffn_w, w_router, b_router, w_mlp1, b_mlp1, w_mlp2, b_mlp2):
    t, d = x.shape
    n_ssd_heads = dt_bias.shape[0]
    d_inner = n_ssd_heads * SSD_HEAD_DIM
    bc_width = 2 * SSD_N_GROUPS * SSD_D_STATE
    conv_dim = d_inner + bc_width
    att_width = w_moba_branch.shape[0]
    n_att_heads = att_width // ATT_HEAD_DIM
    heads_per_group = n_ssd_heads // SSD_N_GROUPS
    n_exp = w_router.shape[1]

    o_z, o_xbc = 0, d_inner
    o_dt = o_xbc + conv_dim
    o_q = o_dt + n_ssd_heads
    o_k, o_v = o_q + att_width, o_q + 2 * att_width
    o_g = o_q + 3 * att_width
    w_dt = w_in[:, o_dt:o_q].reshape(d, SSD_N_GROUPS, heads_per_group)
    w_dt = jnp.pad(w_dt, ((0, 0), (0, 0), (0, LANES - heads_per_group)))
    w_dt = w_dt.reshape(d, SSD_N_GROUPS * LANES)
    w_cat = jnp.concatenate([w_in[:, o_z:o_dt], w_in[:, o_q:], w_dt], axis=1).astype(BF16)
    c_xbc = d_inner
    c_q = c_xbc + conv_dim
    c_k, c_v = c_q + att_width, c_q + 2 * att_width
    c_g = c_q + 3 * att_width
    c_dt = c_g + 2 * d
    proj = norm_matmul(x, norm_mix_w.reshape(1, d), w_cat, tm=1024, tn=1024, out_dtype=F32)

    xbc_act = conv_silu(proj, c_xbc, conv_dim, conv_w, conv_b.reshape(1, conv_dim), tt=512, tc=1024)

    def per_group(v):
        v = v.reshape(SSD_N_GROUPS, 1, heads_per_group)
        return jnp.pad(v, ((0, 0), (0, 0), (0, LANES - heads_per_group)))

    y_ssd = ssd_scan(xbc_act, proj, c_dt // LANES, per_group(dt_bias), per_group(a_log),
                     jnp.repeat(d_skip, SSD_HEAD_DIM).reshape(1, d_inner),
                     ssd_norm_w.reshape(1, d_inner), d_inner)

    half = ATT_HEAD_DIM // 2
    inv_freq = ROPE_THETA ** (-jnp.arange(half, dtype=F32) / half)
    freq2 = jnp.concatenate([inv_freq, inv_freq]).reshape(1, ATT_HEAD_DIM)
    sign2 = jnp.concatenate([-jnp.ones((half,), F32), jnp.ones((half,), F32)]).reshape(1, ATT_HEAD_DIM)
    cos2, sin2 = rope_tables(pos_b.astype(F32).reshape(t, 1), freq2, sign2, tt=2048)
    q_rot, k_ext, v_ext, k_mean = moba_prep(proj, c_q // ATT_HEAD_DIM, c_k // ATT_HEAD_DIM,
                                            c_v // ATT_HEAD_DIM, n_att_heads, cos2, sin2, tt=2048)
    y_moba = moba_attention(q_rot, k_ext, v_ext, k_mean, qt=2048)

    mixed = gated_merge(y_ssd, y_moba, w_ssd_branch.astype(BF16), w_moba_branch.astype(BF16),
                        proj, c_g, tm=512, tn=512)
    x1 = matmul_residual(mixed, w_out.astype(BF16), x, tm=1024, tn=512)

    kv = norm_matmul(mem_b, norm_mem_w.reshape(1, d), w_xkv.astype(BF16),
                     tm=mem_b.shape[0], tn=512, out_dtype=BF16)
    x2 = cross_attention(x1, norm_cross_w.reshape(1, d), w_xq.astype(BF16), kv,
                         w_xo.astype(BF16), tm=256)

    w_rt = w_router.T
    w_rt_hi = w_rt.astype(BF16)
    w_rt_lo = (w_rt - w_rt_hi.astype(F32)).astype(BF16)
    hf, idx_t, gate_t, rank_t, counts = moe_router(x2, norm_ffn_w.reshape(1, d), w_rt_hi, w_rt_lo,
                                                   b_router.reshape(n_exp, 1), tm=512)
    counts = counts[:, 0]
    padded = (counts + MOE_BLOCK - 1) // MOE_BLOCK * MOE_BLOCK
    pad_end = jnp.cumsum(padded)
    pad_start = pad_end - padded
    dest = _lookup(pad_start, idx_t) + rank_t
    n_blk = -(-(t * TOP_K) // MOE_BLOCK) + n_exp
    cap = n_blk * MOE_BLOCK
    blk_row = jnp.arange(n_blk, dtype=I32) * MOE_BLOCK
    blk_expert = jnp.minimum(
        jnp.sum((pad_end[None, :] <= blk_row[:, None]).astype(I32), axis=1), n_exp - 1)
    n_used = pad_end[-1] // MOE_BLOCK
    run_end = _lookup(pad_end, blk_expert) // MOE_BLOCK
    nxt_expert = jnp.where(run_end < n_used,
                           _lookup(blk_expert, jnp.minimum(run_end, n_blk - 1)), -1)
    n_used = n_used.reshape(1)
    xb = moe_dispatch(pad_end, n_used, dest, hf, cap, tt=256)
    act = expert_up(blk_expert, nxt_expert, n_used, xb, w_mlp1, b_mlp1.reshape(n_exp, 1, -1),
                    tf=1024)
    yb = expert_down(blk_expert, nxt_expert, n_used, act, w_mlp2, b_mlp2.reshape(n_exp, 1, d))
    return dict(proj=proj, xbc_act=xbc_act, y_ssd=y_ssd, y_moba=y_moba, mixed=mixed, x1=x1, x2=x2,
                hf=hf, idx_t=idx_t, gate_t=gate_t, dest=dest, yb=yb)


def kernel(x, mem, positions, norm_mix_w, w_in, conv_w, conv_b, dt_bias, a_log, d_skip, ssd_norm_w,
           w_ssd_branch, w_moba_branch, w_out, norm_cross_w, norm_mem_w, w_xq, w_xkv, w_xo,
           norm_ffn_w, w_router, b_router, w_mlp1, b_mlp1, w_mlp2, b_mlp2, norm_final_w):
    assert w_in.shape[0] == 1, "depth-1 trunk only"
    d = x.shape[-1]
    outs = []
    for b in range(x.shape[0]):
        s = _layer(x[b], mem[b], positions[b], norm_mix_w[0], w_in[0], conv_w[0], conv_b[0],
                   dt_bias[0], a_log[0], d_skip[0], ssd_norm_w[0], w_ssd_branch[0],
                   w_moba_branch[0], w_out[0], norm_cross_w[0], norm_mem_w[0], w_xq[0], w_xkv[0],
                   w_xo[0], norm_ffn_w[0], w_router[0], b_router[0], w_mlp1[0], b_mlp1[0],
                   w_mlp2[0], b_mlp2[0])
        outs.append(moe_combine(s["dest"], s["x2"], s["gate_t"].T, norm_final_w.reshape(1, d),
                                s["yb"], tt=128))
    return jnp.stack(outs, axis=0)
```

```python
import functools

import jax
import jax.numpy as jnp
from jax import lax
from jax.experimental import pallas as pl
from jax.experimental.pallas import tpu as pltpu

F32 = jnp.float32
BF16 = jnp.bfloat16
I32 = jnp.int32

EPS = 1e-5
SSD_HEAD_DIM = 64
SSD_N_GROUPS = 8
SSD_D_STATE = 128
SSD_CONV_WIDTH = 4
SSD_CHUNK = 256
ATT_HEAD_DIM = 128
MOBA_BLOCK = 256
MOBA_TOPK = 3
ROPE_THETA = 10000.0
XATT_N_HEADS = 4
TOP_K = 4
SWIGLU_ALPHA = 1.702
SWIGLU_LIMIT = 7.0
MOE_BLOCK = 256

LANES = 128
VMEM_LIMIT = 56 * 1024 * 1024
MASKED = -1e30

NT_DIMS = (((1,), (1,)), ((), ()))


def _params(n_axes):
    return pltpu.CompilerParams(dimension_semantics=("arbitrary",) * n_axes,
                                vmem_limit_bytes=VMEM_LIMIT)


def _dot(a, b):
    return jnp.dot(a, b, preferred_element_type=F32)


def _dot_nt(a, b):
    return lax.dot_general(a, b, NT_DIMS, preferred_element_type=F32)


def _split3(v):
    hi = v.astype(BF16)
    r = v - hi.astype(F32)
    mid = r.astype(BF16)
    lo = (r - mid.astype(F32)).astype(BF16)
    return hi, mid, lo


def _dot3_rhs_exact(a, b_exact):
    hi, mid, lo = _split3(a)
    return _dot(hi, b_exact) + _dot(mid, b_exact) + _dot(lo, b_exact)


def _dot3_lhs_exact(a_exact, b):
    hi, mid, lo = _split3(b)
    return _dot(a_exact, hi) + _dot(a_exact, mid) + _dot(a_exact, lo)


def _rms(x, g):
    return x * lax.rsqrt(jnp.mean(x * x, axis=-1, keepdims=True) + EPS) * g


def _sigmoid(x):
    return 1.0 / (1.0 + jnp.exp(-x))


def _norm_matmul_kernel(x_ref, g_ref, w_ref, o_ref, h_ref):
    @pl.when(pl.program_id(1) == 0)
    def _():
        h_ref[...] = _rms(x_ref[...], g_ref[...]).astype(BF16)

    o_ref[...] = _dot(h_ref[...], w_ref[...]).astype(o_ref.dtype)


def norm_matmul(x, g, w, tm, tn, out_dtype):
    m, k = x.shape
    n = w.shape[1]
    return pl.pallas_call(
        _norm_matmul_kernel,
        out_shape=jax.ShapeDtypeStruct((m, n), out_dtype),
        grid=(m // tm, n // tn),
        in_specs=[pl.BlockSpec((tm, k), lambda i, j: (i, 0)),
                  pl.BlockSpec((1, k), lambda i, j: (0, 0)),
                  pl.BlockSpec((k, tn), lambda i, j: (0, j))],
        out_specs=pl.BlockSpec((tm, tn), lambda i, j: (i, j)),
        scratch_shapes=[pltpu.VMEM((tm, k), BF16)],
        compiler_params=_params(2),
        name="norm_matmul",
    )(x, g, w)


def _norm_matmul2_kernel(x_ref, g_ref, wa_ref, wb_ref, o_ref, h_ref, *, n_a):
    j = pl.program_id(1)

    @pl.when(j == 0)
    def _():
        h_ref[...] = _rms(x_ref[...], g_ref[...]).astype(BF16)

    @pl.when(j < n_a)
    def _():
        o_ref[...] = _dot(h_ref[...], wa_ref[...]).astype(o_ref.dtype)

    @pl.when(j >= n_a)
    def _():
        o_ref[...] = _dot(h_ref[...], wb_ref[...]).astype(o_ref.dtype)


def norm_matmul2(x, g, w_a, w_b, tm, tn, out_dtype):
    m, k = x.shape
    n_a = w_a.shape[1] // tn
    n_b = w_b.shape[1] // tn
    return pl.pallas_call(
        functools.partial(_norm_matmul2_kernel, n_a=n_a),
        out_shape=jax.ShapeDtypeStruct((m, (n_a + n_b) * tn), out_dtype),
        grid=(m // tm, n_a + n_b),
        in_specs=[pl.BlockSpec((tm, k), lambda i, j: (i, 0)),
                  pl.BlockSpec((1, k), lambda i, j: (0, 0)),
                  pl.BlockSpec((k, tn), lambda i, j: (0, jnp.minimum(j, n_a - 1))),
                  pl.BlockSpec((k, tn), lambda i, j: (0, jnp.maximum(j - n_a, 0)))],
        out_specs=pl.BlockSpec((tm, tn), lambda i, j: (i, j)),
        scratch_shapes=[pltpu.VMEM((tm, k), BF16)],
        compiler_params=_params(2),
        name="in_proj",
    )(x, g, w_a, w_b)


def _rope_table_kernel(pos_ref, freq_ref, sign_ref, cos_ref, sin_ref):
    ang = pos_ref[...] * freq_ref[...]
    cos_ref[...] = jnp.cos(ang)
    sin_ref[...] = jnp.sin(ang) * sign_ref[...]


def rope_tables(pos_col, freq2, sign2, tt):
    t = pos_col.shape[0]
    d = freq2.shape[1]
    out = jax.ShapeDtypeStruct((t, d), F32)
    return pl.pallas_call(
        _rope_table_kernel,
        out_shape=(out, out),
        grid=(t // tt,),
        in_specs=[pl.BlockSpec((tt, 1), lambda i: (i, 0)),
                  pl.BlockSpec((1, d), lambda i: (0, 0)),
                  pl.BlockSpec((1, d), lambda i: (0, 0))],
        out_specs=(pl.BlockSpec((tt, d), lambda i: (i, 0)),
                   pl.BlockSpec((tt, d), lambda i: (i, 0))),
        compiler_params=_params(1),
        name="rope_tables",
    )(pos_col, freq2, sign2)


def _conv_silu_kernel(u_ref, w_ref, b_ref, o_ref, ext_ref, *, tt):
    halo = 8

    @pl.when(pl.program_id(1) == 0)
    def _():
        ext_ref[0:halo, :] = jnp.zeros((halo, ext_ref.shape[1]), F32)

    ext_ref[halo:halo + tt, :] = u_ref[...]
    acc = b_ref[...]
    for i in range(SSD_CONV_WIDTH):
        off = halo - (SSD_CONV_WIDTH - 1) + i
        acc = acc + w_ref[i:i + 1, :] * ext_ref[off:off + tt, :]
    o_ref[...] = acc * _sigmoid(acc)
    ext_ref[0:halo, :] = ext_ref[tt:tt + halo, :]


def conv_silu(proj, col_off, width, conv_w, conv_b, tt, tc):
    t = proj.shape[0]
    off_blk = col_off // tc
    return pl.pallas_call(
        functools.partial(_conv_silu_kernel, tt=tt),
        out_shape=jax.ShapeDtypeStruct((t, width), F32),
        grid=(width // tc, t // tt),
        in_specs=[pl.BlockSpec((tt, tc), lambda j, i: (i, off_blk + j)),
                  pl.BlockSpec((SSD_CONV_WIDTH, tc), lambda j, i: (0, j)),
                  pl.BlockSpec((1, tc), lambda j, i: (0, j))],
        out_specs=pl.BlockSpec((tt, tc), lambda j, i: (i, j)),
        scratch_shapes=[pltpu.VMEM((tt + 8, tc), F32)],
        compiler_params=_params(2),
        name="conv_silu",
    )(proj, conv_w, conv_b)


def _ssd_kernel(x_ref, b_ref, c_ref, dt_ref, z_ref, bias_ref, alog_ref, dskip_ref, nw_ref,
                o_ref, state_ref, *, heads_per_group):
    chunk = x_ref.shape[0]
    gw = x_ref.shape[1]
    g = pl.program_id(1)

    @pl.when(pl.program_id(0) == 0)
    def _():
        state_ref[g] = jnp.zeros(state_ref.shape[1:], F32)

    raw = dt_ref[...] + bias_ref[...]
    dt = jnp.maximum(raw, 0.0) + jnp.log1p(jnp.exp(-jnp.abs(raw)))
    a = -jnp.exp(alog_ref[...])
    da = dt * a
    row = lax.broadcasted_iota(I32, (chunk, chunk), 0)
    col = lax.broadcasted_iota(I32, (chunk, chunk), 1)
    causal = row >= col
    tril = jnp.where(causal, 1.0, 0.0).astype(BF16)
    acum = _dot3_lhs_exact(tril, da)
    acum_t = acum.T
    a_last = acum[chunk - 1:chunk, :]
    exp_acum = jnp.exp(acum)
    to_end = jnp.exp(a_last - acum) * dt
    exp_last = jnp.broadcast_to(jnp.exp(a_last), (8, LANES))

    hsel = lax.broadcasted_iota(I32, (LANES, gw), 0)
    lane_head = lax.broadcasted_iota(I32, (LANES, gw), 1) // SSD_HEAD_DIM
    expand = jnp.where(hsel == lane_head, 1.0, 0.0).astype(BF16)
    stacked = jnp.concatenate([dt, to_end, exp_acum, exp_last], axis=0)
    spread = _dot3_rhs_exact(stacked, expand)
    dt_x = spread[0:chunk]
    to_end_x = spread[chunk:2 * chunk]
    exp_acum_x = spread[2 * chunk:3 * chunk]
    exp_last_x = spread[3 * chunk:3 * chunk + 1]

    x = x_ref[...]
    bmat = b_ref[...]
    cb16 = c_ref[...].astype(BF16)
    bb16 = bmat.astype(BF16)
    cb = _dot_nt(cb16, bb16)
    xdt = (x * dt_x).astype(BF16)
    state_t = state_ref[g]
    y_off = _dot(cb16, state_t.astype(BF16)) * exp_acum_x

    pieces = []
    for j in range(heads_per_group):
        seg = acum[:, j:j + 1] - acum_t[j:j + 1, :]
        decay = jnp.where(causal, jnp.exp(seg), 0.0)
        w = (cb * decay).astype(BF16)
        pieces.append(_dot(w, xdt[:, j * SSD_HEAD_DIM:(j + 1) * SSD_HEAD_DIM]))
    y_diag = jnp.concatenate(pieces, axis=1)

    bt16 = bmat.T.astype(BF16)
    state_ref[g] = state_t * exp_last_x + _dot(bt16, (x * to_end_x).astype(BF16))

    y = y_diag + y_off + dskip_ref[...] * x
    z = z_ref[...]
    u = y * (z * _sigmoid(z))
    u = u * lax.rsqrt(jnp.mean(u * u, axis=-1, keepdims=True) + EPS)
    o_ref[...] = (u * nw_ref[...]).astype(o_ref.dtype)


def ssd_scan(xbc_act, proj, dt_col_blk, bias_g, alog_g, dskip_x, norm_w, d_inner):
    t = xbc_act.shape[0]
    gw = d_inner // SSD_N_GROUPS
    heads_per_group = gw // SSD_HEAD_DIM
    b_blk = d_inner // SSD_D_STATE
    c_blk = b_blk + SSD_N_GROUPS
    return pl.pallas_call(
        functools.partial(_ssd_kernel, heads_per_group=heads_per_group),
        out_shape=jax.ShapeDtypeStruct((t, d_inner), BF16),
        grid=(t // SSD_CHUNK, SSD_N_GROUPS),
        in_specs=[pl.BlockSpec((SSD_CHUNK, gw), lambda c, g: (c, g)),
                  pl.BlockSpec((SSD_CHUNK, SSD_D_STATE), lambda c, g: (c, b_blk + g)),
                  pl.BlockSpec((SSD_CHUNK, SSD_D_STATE), lambda c, g: (c, c_blk + g)),
                  pl.BlockSpec((SSD_CHUNK, LANES), lambda c, g: (c, dt_col_blk + g)),
                  pl.BlockSpec((SSD_CHUNK, gw), lambda c, g: (c, g)),
                  pl.BlockSpec((None, 1, LANES), lambda c, g: (g, 0, 0)),
                  pl.BlockSpec((None, 1, LANES), lambda c, g: (g, 0, 0)),
                  pl.BlockSpec((1, gw), lambda c, g: (0, g)),
                  pl.BlockSpec((1, gw), lambda c, g: (0, g))],
        out_specs=pl.BlockSpec((SSD_CHUNK, gw), lambda c, g: (c, g)),
        scratch_shapes=[pltpu.VMEM((SSD_N_GROUPS, SSD_D_STATE, gw), F32)],
        compiler_params=_params(2),
        name="ssd_scan",
    )(xbc_act, xbc_act, xbc_act, proj, proj, bias_g, alog_g, dskip_x, norm_w)


V_EXT_ROWS = ATT_HEAD_DIM + 16


def _moba_prep_kernel(q_ref, k_ref, v_ref, cos_ref, sin_ref, qo_ref, ke_ref, ve_ref, km_ref):
    cos = cos_ref[...]
    sin = sin_ref[...]
    half = ATT_HEAD_DIM // 2
    q = q_ref[...]
    k = k_ref[...]
    qr = q * cos + pltpu.roll(q, half, 1) * sin
    kr = k * cos + pltpu.roll(k, half, 1) * sin
    qo_ref[...] = qr.astype(BF16)
    n_blk = ke_ref.shape[0]
    first_blk = pl.program_id(1) * n_blk
    kr3 = kr.reshape(n_blk, MOBA_BLOCK, ATT_HEAD_DIM)
    km_ref[...] = jnp.mean(kr3, axis=1)
    lane = lax.broadcasted_iota(I32, (MOBA_BLOCK, LANES), 1)
    pad_row = lax.broadcasted_iota(I32, (V_EXT_ROWS - ATT_HEAD_DIM, MOBA_BLOCK), 0)
    ones_rows = jnp.where(pad_row == 0, 1.0, 0.0).astype(BF16)
    for b in range(n_blk):
        tag = jnp.where(lane == first_blk + b, 1.0, 0.0).astype(BF16)
        ke_ref[b] = jnp.concatenate([kr3[b].astype(BF16), tag], axis=1)
        v_t = v_ref[b * MOBA_BLOCK:(b + 1) * MOBA_BLOCK, :].T.astype(BF16)
        ve_ref[b] = jnp.concatenate([v_t, ones_rows], axis=0)


def moba_prep(proj, q_blk, k_blk, v_blk, n_heads, cos2, sin2, tt):
    t = proj.shape[0]
    nb = t // MOBA_BLOCK
    assert nb <= LANES, "block tags live in one 128-lane tile"
    bps = tt // MOBA_BLOCK
    d = ATT_HEAD_DIM
    return pl.pallas_call(
        _moba_prep_kernel,
        out_shape=(jax.ShapeDtypeStruct((n_heads, t, d), BF16),
                   jax.ShapeDtypeStruct((n_heads, nb, MOBA_BLOCK, d + LANES), BF16),
                   jax.ShapeDtypeStruct((n_heads, nb, V_EXT_ROWS, MOBA_BLOCK), BF16),
                   jax.ShapeDtypeStruct((n_heads, nb, d), F32)),
        grid=(n_heads, t // tt),
        in_specs=[pl.BlockSpec((tt, d), lambda h, i: (i, q_blk + h)),
                  pl.BlockSpec((tt, d), lambda h, i: (i, k_blk + h)),
                  pl.BlockSpec((tt, d), lambda h, i: (i, v_blk + h)),
                  pl.BlockSpec((tt, d), lambda h, i: (i, 0)),
                  pl.BlockSpec((tt, d), lambda h, i: (i, 0))],
        out_specs=(pl.BlockSpec((None, tt, d), lambda h, i: (h, i, 0)),
                   pl.BlockSpec((None, bps, MOBA_BLOCK, d + LANES), lambda h, i: (h, i, 0, 0)),
                   pl.BlockSpec((None, bps, V_EXT_ROWS, MOBA_BLOCK), lambda h, i: (h, i, 0, 0)),
                   pl.BlockSpec((None, bps, d), lambda h, i: (h, i, 0))),
        compiler_params=_params(2),
        name="moba_prep",
    )(proj, proj, proj, cos2, sin2)


M_INIT = -3.0e38
LOG2_E = 1.4426950408889634


def _moba_attn_kernel(q_ref, ke_ref, ve_ref, km_ref, o_ref, qe_ref, acc_ref, sa_ref, sb_ref, *,
                      blocks_per_tile):
    i = pl.program_id(1)
    nb = ke_ref.shape[0]
    qt = q_ref.shape[0]
    first = i * blocks_per_tile
    q = q_ref[...]

    km = km_ref[...].astype(BF16)
    if nb < LANES:
        km = jnp.concatenate([km, jnp.zeros((LANES - nb, ATT_HEAD_DIM), BF16)], axis=0)
    gate = _dot_nt(km, q)
    blk = lax.broadcasted_iota(I32, gate.shape, 0)
    own = first + lax.broadcasted_iota(I32, gate.shape, 1) // MOBA_BLOCK
    past = blk < own
    g = jnp.where(past, gate, -jnp.inf)
    chosen = jnp.zeros(gate.shape, F32)
    for _ in range(MOBA_TOPK):
        best = jnp.max(g, axis=0, keepdims=True)
        first_best = jnp.min(jnp.where(g == best, blk, LANES), axis=0, keepdims=True)
        pick = blk == first_best
        chosen = jnp.where(pick, 1.0, chosen)
        g = jnp.where(pick, -jnp.inf, g)
    visible = jnp.where(past, chosen, jnp.where(blk == own, 1.0, 0.0))
    tags = jnp.where(visible > 0.5, 0.0, MASKED).T
    qe_ref[...] = jnp.concatenate([q, tags.astype(BF16)], axis=1)

    c = (ATT_HEAD_DIM ** -0.5) * LOG2_E
    acc_ref[...] = jnp.zeros(acc_ref.shape, F32)

    def qk(j, lo):
        return _dot_nt(ke_ref[j], qe_ref[lo:, :])

    def softmax_pv(s, j, m, lo, causal):
        if causal:
            width = qt - lo
            key_pos = lax.broadcasted_iota(I32, (MOBA_BLOCK, width), 0)
            col_pos = lax.broadcasted_iota(I32, (MOBA_BLOCK, width), 1)
            s = jnp.where(key_pos <= col_pos, s, MASKED)
        m_new = jnp.maximum(m, jnp.max(s, axis=0, keepdims=True))
        alpha = jnp.exp2((m - m_new) * c)
        p = jnp.exp2((s - m_new) * c).astype(BF16)
        acc_ref[:, lo:] = alpha * acc_ref[:, lo:] + _dot(ve_ref[j], p)
        return m_new

    sa_ref[...] = qk(0, 0)

    def pair(jj, m):
        j = 2 * jj
        sb_ref[...] = qk(j + 1, 0)
        m = softmax_pv(sa_ref[...], j, m, 0, False)
        sa_ref[...] = qk(j + 2, 0)
        return softmax_pv(sb_ref[...], j + 1, m, 0, False)

    m = jnp.full((1, qt), M_INIT, F32)
    m = lax.fori_loop(0, first // 2, pair, m)
    bufs = (sa_ref, sb_ref)
    for dd in range(blocks_per_tile):
        lo = dd * MOBA_BLOCK
        if dd + 1 < blocks_per_tile:
            bufs[(dd + 1) % 2][:, lo + MOBA_BLOCK:] = qk(first + dd + 1, lo + MOBA_BLOCK)
        m_new = softmax_pv(bufs[dd % 2][:, lo:], first + dd, m[:, lo:], lo, True)
        m = m_new if lo == 0 else jnp.concatenate([m[:, :lo], m_new], axis=1)
    acc = acc_ref[...]
    out_t = acc[0:ATT_HEAD_DIM] / acc[ATT_HEAD_DIM:ATT_HEAD_DIM + 1]
    o_ref[...] = out_t.T.astype(o_ref.dtype)


def moba_attention(q_rot, k_ext, v_ext, k_mean, qt):
    n_heads, t, d = q_rot.shape
    nb = t // MOBA_BLOCK
    assert (qt // MOBA_BLOCK) % 2 == 0, "past blocks are consumed in pairs"
    return pl.pallas_call(
        functools.partial(_moba_attn_kernel, blocks_per_tile=qt // MOBA_BLOCK),
        out_shape=jax.ShapeDtypeStruct((t, n_heads * d), BF16),
        grid=(n_heads, t // qt),
        in_specs=[pl.BlockSpec((None, qt, d), lambda h, i: (h, i, 0)),
                  pl.BlockSpec((None, nb, MOBA_BLOCK, d + LANES), lambda h, i: (h, 0, 0, 0)),
                  pl.BlockSpec((None, nb, V_EXT_ROWS, MOBA_BLOCK), lambda h, i: (h, 0, 0, 0)),
                  pl.BlockSpec((None, nb, d), lambda h, i: (h, 0, 0))],
        out_specs=pl.BlockSpec((qt, d), lambda h, i: (i, h)),
        scratch_shapes=[pltpu.VMEM((qt, d + LANES), BF16),
                        pltpu.VMEM((V_EXT_ROWS, qt), F32),
                        pltpu.VMEM((MOBA_BLOCK, qt), F32),
                        pltpu.VMEM((MOBA_BLOCK, qt), F32)],
        compiler_params=_params(2),
        name="moba_attention",
    )(q_rot, k_ext, v_ext, k_mean)


def _gated_merge_kernel(ys_ref, ym_ref, ws_ref, wm_ref, gs_ref, gm_ref, o_ref):
    a = _dot(ys_ref[...], ws_ref[...])
    b = _dot(ym_ref[...], wm_ref[...])
    o_ref[...] = (_sigmoid(gs_ref[...]) * a + _sigmoid(gm_ref[...]) * b).astype(o_ref.dtype)


def gated_merge(y_ssd, y_moba, w_s, w_m, proj, gate_col_off, tm, tn):
    t = y_ssd.shape[0]
    n = w_s.shape[1]
    gs_blk = gate_col_off // tn
    gm_blk = gs_blk + n // tn
    return pl.pallas_call(
        _gated_merge_kernel,
        out_shape=jax.ShapeDtypeStruct((t, n), BF16),
        grid=(t // tm, n // tn),
        in_specs=[pl.BlockSpec((tm, y_ssd.shape[1]), lambda i, j: (i, 0)),
                  pl.BlockSpec((tm, y_moba.shape[1]), lambda i, j: (i, 0)),
                  pl.BlockSpec((w_s.shape[0], tn), lambda i, j: (0, j)),
                  pl.BlockSpec((w_m.shape[0], tn), lambda i, j: (0, j)),
                  pl.BlockSpec((tm, tn), lambda i, j: (i, gs_blk + j)),
                  pl.BlockSpec((tm, tn), lambda i, j: (i, gm_blk + j))],
        out_specs=pl.BlockSpec((tm, tn), lambda i, j: (i, j)),
        compiler_params=_params(2),
        name="gated_merge",
    )(y_ssd, y_moba, w_s, w_m, proj, proj)


def _matmul_residual_kernel(a_ref, w_ref, r_ref, o_ref):
    o_ref[...] = r_ref[...] + _dot(a_ref[...], w_ref[...])


def matmul_residual(a, w, res, tm, tn):
    m, k = a.shape
    n = w.shape[1]
    return pl.pallas_call(
        _matmul_residual_kernel,
        out_shape=jax.ShapeDtypeStruct((m, n), F32),
        grid=(m // tm, n // tn),
        in_specs=[pl.BlockSpec((tm, k), lambda i, j: (i, 0)),
                  pl.BlockSpec((k, tn), lambda i, j: (0, j)),
                  pl.BlockSpec((tm, tn), lambda i, j: (i, j))],
        out_specs=pl.BlockSpec((tm, tn), lambda i, j: (i, j)),
        compiler_params=_params(2),
        name="matmul_residual",
    )(a, w, res)


def _cross_attn_kernel(x_ref, g_ref, wq_ref, kv_ref, wo_ref, o_ref):
    x = x_ref[...]
    d = x.shape[1]
    hd = d // XATT_N_HEADS
    h = _rms(x, g_ref[...]).astype(BF16)
    q = _dot(h, wq_ref[...]).astype(BF16)
    outs = []
    for hh in range(XATT_N_HEADS):
        qh = q[:, hh * hd:(hh + 1) * hd]
        kh = kv_ref[:, hh * hd:(hh + 1) * hd]
        vh = kv_ref[:, d + hh * hd:d + (hh + 1) * hd]
        s = _dot_nt(qh, kh) * (hd ** -0.5)
        p = jnp.exp(s - jnp.max(s, axis=-1, keepdims=True))
        p = p / jnp.sum(p, axis=-1, keepdims=True)
        outs.append(_dot(p.astype(BF16), vh))
    o = jnp.concatenate(outs, axis=1).astype(BF16)
    o_ref[...] = x + _dot(o, wo_ref[...])


def cross_attention(x, g, w_q, kv, w_o, tm):
    t, d = x.shape
    const = lambda i: (0, 0)
    single = pl.Buffered(1)
    return pl.pallas_call(
        _cross_attn_kernel,
        out_shape=jax.ShapeDtypeStruct((t, d), F32),
        grid=(t // tm,),
        in_specs=[pl.BlockSpec((tm, d), lambda i: (i, 0)),
                  pl.BlockSpec((1, d), const),
                  pl.BlockSpec(w_q.shape, const, pipeline_mode=single),
                  pl.BlockSpec(kv.shape, const, pipeline_mode=single),
                  pl.BlockSpec(w_o.shape, const, pipeline_mode=single)],
        out_specs=pl.BlockSpec((tm, d), lambda i: (i, 0)),
        compiler_params=_params(1),
        name="cross_attention",
    )(x, g, w_q, kv, w_o)


def _router_kernel(x_ref, g_ref, whi_ref, wlo_ref, b_ref, h_ref, idx_ref, gate_ref, rank_ref,
                   cnt_ref, run_ref):
    tm = x_ref.shape[0]
    n_exp = whi_ref.shape[0]

    @pl.when(pl.program_id(0) == 0)
    def _():
        run_ref[...] = jnp.zeros(run_ref.shape, F32)

    h = _rms(x_ref[...], g_ref[...])
    h_ref[...] = h
    hi = h.astype(BF16)
    lo = (h - hi.astype(F32)).astype(BF16)
    logits = (_dot_nt(whi_ref[...], hi) + _dot_nt(whi_ref[...], lo) + _dot_nt(wlo_ref[...], hi)
              + b_ref[...])

    eid = lax.broadcasted_iota(I32, logits.shape, 0)
    g = logits
    member = jnp.zeros(logits.shape, F32)
    idxs, vals, picks = [], [], []
    for _ in range(TOP_K):
        best = jnp.max(g, axis=0, keepdims=True)
        first = jnp.min(jnp.where(g == best, eid, n_exp), axis=0, keepdims=True)
        pick = eid == first
        member = jnp.where(pick, 1.0, member)
        g = jnp.where(pick, -jnp.inf, g)
        idxs.append(first)
        vals.append(best)
        picks.append(pick)
    exps = [jnp.exp(v - vals[0]) for v in vals]
    denom = exps[0]
    for e in exps[1:]:
        denom = denom + e

    src = lax.broadcasted_iota(I32, (tm, tm), 0)
    dst = lax.broadcasted_iota(I32, (tm, tm), 1)
    before = jnp.where(src < dst, 1.0, 0.0).astype(BF16)
    run = run_ref[...]
    rank_all = _dot(member.astype(BF16), before) + run[:, 0:1]
    ranks = [jnp.sum(jnp.where(p, rank_all, 0.0), axis=0, keepdims=True) for p in picks]

    idx_ref[...] = jnp.concatenate(idxs, axis=0)
    gate_ref[...] = jnp.concatenate([e / denom for e in exps], axis=0)
    rank_ref[...] = jnp.concatenate(ranks, axis=0).astype(I32)
    run = run + jnp.sum(member, axis=1, keepdims=True)
    run_ref[...] = run
    cnt_ref[...] = run.astype(I32)


def moe_router(x, g, w_hi, w_lo, b_col, tm):
    t, d = x.shape
    n_exp = w_hi.shape[0]
    const = lambda i: (0, 0)
    return pl.pallas_call(
        _router_kernel,
        out_shape=(jax.ShapeDtypeStruct((t, d), F32),
                   jax.ShapeDtypeStruct((TOP_K, t), I32),
                   jax.ShapeDtypeStruct((TOP_K, t), F32),
                   jax.ShapeDtypeStruct((TOP_K, t), I32),
                   jax.ShapeDtypeStruct((n_exp, LANES), I32)),
        grid=(t // tm,),
        in_specs=[pl.BlockSpec((tm, d), lambda i: (i, 0)),
                  pl.BlockSpec((1, d), const),
                  pl.BlockSpec((n_exp, d), const),
                  pl.BlockSpec((n_exp, d), const),
                  pl.BlockSpec((n_exp, 1), const)],
        out_specs=(pl.BlockSpec((tm, d), lambda i: (i, 0)),
                   pl.BlockSpec((TOP_K, tm), lambda i: (0, i)),
                   pl.BlockSpec((TOP_K, tm), lambda i: (0, i)),
                   pl.BlockSpec((TOP_K, tm), lambda i: (0, i)),
                   pl.BlockSpec((n_exp, LANES), const)),
        scratch_shapes=[pltpu.VMEM((n_exp, LANES), F32)],
        compiler_params=_params(1),
        name="moe_router",
    )(x, g, w_hi, w_lo, b_col)


def _dispatch_kernel(pe_ref, nu_ref, dest_ref, h_ref, buf_ref, zero_ref, sem, zero_sem, *, n_blk):
    tt = h_ref.shape[0]
    n_exp = pe_ref.shape[0]

    @pl.when(pl.program_id(0) == 0)
    def _():
        zero_ref[...] = jnp.zeros(zero_ref.shape, zero_ref.dtype)

        def zero_block(row0):
            return pltpu.make_async_copy(zero_ref, buf_ref.at[pl.ds(row0, MOE_BLOCK), :], zero_sem)

        def last_block(e):
            return pl.multiple_of(jnp.maximum(pe_ref[e] - MOE_BLOCK, 0), MOE_BLOCK)

        for e in range(n_exp):
            zero_block(last_block(e)).start()

        def tail_start(b, carry):
            zero_block(pl.multiple_of(b * MOE_BLOCK, MOE_BLOCK)).start()
            return carry

        def tail_wait(b, carry):
            zero_block(pl.multiple_of(b * MOE_BLOCK, MOE_BLOCK)).wait()
            return carry

        lax.fori_loop(nu_ref[0], n_blk, tail_start, 0)
        for e in range(n_exp):
            zero_block(last_block(e)).wait()
        lax.fori_loop(nu_ref[0], n_blk, tail_wait, 0)

    def row_copy(r, slot):
        return pltpu.make_async_copy(h_ref.at[pl.ds(r, 1), :], buf_ref.at[pl.ds(slot, 1), :], sem)

    def start(r, carry):
        for k in range(TOP_K):
            row_copy(r, dest_ref[k, r]).start(priority=k % 2)
        return carry

    def wait(r, carry):
        for k in range(TOP_K):
            row_copy(r, dest_ref[k, r]).wait()
        return carry

    lax.fori_loop(0, tt, start, 0)
    lax.fori_loop(0, tt, wait, 0)


def moe_dispatch(pad_end, n_used, dest, h, cap, tt):
    t, d = h.shape
    grid_spec = pltpu.PrefetchScalarGridSpec(
        num_scalar_prefetch=2,
        grid=(t // tt,),
        in_specs=[pl.BlockSpec((TOP_K, tt), lambda i, pe, nu: (0, i), memory_space=pltpu.SMEM),
                  pl.BlockSpec((tt, d), lambda i, pe, nu: (i, 0))],
        out_specs=pl.BlockSpec(memory_space=pl.ANY),
        scratch_shapes=[pltpu.VMEM((MOE_BLOCK, d), h.dtype),
                        pltpu.SemaphoreType.DMA(()),
                        pltpu.SemaphoreType.DMA(())],
    )
    return pl.pallas_call(
        functools.partial(_dispatch_kernel, n_blk=cap // MOE_BLOCK),
        out_shape=jax.ShapeDtypeStruct((cap, d), h.dtype),
        grid_spec=grid_spec,
        compiler_params=_params(1),
        name="moe_dispatch",
    )(pad_end, n_used, dest, h)


def _run_start(b, be_ref):
    return (b == 0) | (be_ref[b] != be_ref[jnp.maximum(b - 1, 0)])


def _expert_up_kernel(be_ref, nx_ref, nu_ref, x_ref, w_hbm, bg_ref, bl_ref, o_ref,
                      wbuf, w16, sem, slot_ref, *, tf, n_f):
    f = pl.program_id(0)
    b = pl.program_id(1)
    e = be_ref[b]

    def fetch(expert, slot):
        copies = []
        for half in range(2):
            col = pl.multiple_of((half * n_f + f) * tf, tf)
            copies.append(pltpu.make_async_copy(w_hbm.at[expert, :, pl.ds(col, tf)],
                                                wbuf.at[slot, half], sem.at[slot, half]))
        return copies

    @pl.when(b < nu_ref[0])
    def _():
        @pl.when(b == 0)
        def _():
            slot_ref[0] = 0
            for cp in fetch(e, 0):
                cp.start()

        @pl.when(_run_start(b, be_ref))
        def _():
            slot = slot_ref[0]
            for cp in fetch(e, slot):
                cp.wait()
            nxt = nx_ref[b]

            @pl.when(nxt >= 0)
            def _():
                for cp in fetch(nxt, 1 - slot):
                    cp.start()

            for half in range(2):
                w16[half] = wbuf[slot, half].astype(BF16)
            slot_ref[0] = 1 - slot

        x = x_ref[...].astype(BF16)
        glu = _dot(x, w16[0]) + bg_ref[...]
        lin = _dot(x, w16[1]) + bl_ref[...]
        glu = jnp.minimum(glu, SWIGLU_LIMIT)
        lin = jnp.clip(lin, -SWIGLU_LIMIT, SWIGLU_LIMIT)
        o_ref[...] = (glu * _sigmoid(SWIGLU_ALPHA * glu) * (lin + 1.0)).astype(o_ref.dtype)

    @pl.when(b >= nu_ref[0])
    def _():
        o_ref[...] = jnp.zeros(o_ref.shape, o_ref.dtype)


def expert_up(blk_expert, nxt_expert, n_used, xb, w1, b1, tf):
    cap, d = xb.shape
    d_ff = w1.shape[2] // 2
    n_f = d_ff // tf
    n_blk = cap // MOE_BLOCK

    def blk(b, nu):
        return jnp.minimum(b, nu[0] - 1)

    grid_spec = pltpu.PrefetchScalarGridSpec(
        num_scalar_prefetch=3,
        grid=(n_f, n_blk),
        in_specs=[pl.BlockSpec((MOE_BLOCK, d), lambda f, b, be, nx, nu: (blk(b, nu), 0)),
                  pl.BlockSpec(memory_space=pl.ANY),
                  pl.BlockSpec((None, 1, tf), lambda f, b, be, nx, nu: (be[blk(b, nu)], 0, f)),
                  pl.BlockSpec((None, 1, tf),
                               lambda f, b, be, nx, nu: (be[blk(b, nu)], 0, n_f + f))],
        out_specs=pl.BlockSpec((MOE_BLOCK, tf), lambda f, b, be, nx, nu: (b, f)),
        scratch_shapes=[pltpu.VMEM((2, 2, d, tf), F32),
                        pltpu.VMEM((2, d, tf), BF16),
                        pltpu.SemaphoreType.DMA((2, 2)),
                        pltpu.SMEM((1,), I32)],
    )
    return pl.pallas_call(
        functools.partial(_expert_up_kernel, tf=tf, n_f=n_f),
        out_shape=jax.ShapeDtypeStruct((cap, d_ff), BF16),
        grid_spec=grid_spec,
        compiler_params=_params(2),
        name="expert_up",
    )(blk_expert, nxt_expert, n_used, xb, w1, b1, b1)


def _expert_down_kernel(be_ref, nx_ref, nu_ref, a_ref, w_hbm, b_ref, o_ref, wbuf, w16, sem, slot_ref):
    b = pl.program_id(0)
    e = be_ref[b]

    def fetch(expert, slot):
        return pltpu.make_async_copy(w_hbm.at[expert], wbuf.at[slot], sem.at[slot])

    @pl.when(b < nu_ref[0])
    def _():
        @pl.when(b == 0)
        def _():
            slot_ref[0] = 0
            fetch(e, 0).start()

        @pl.when(_run_start(b, be_ref))
        def _():
            slot = slot_ref[0]
            fetch(e, slot).wait()
            nxt = nx_ref[b]

            @pl.when(nxt >= 0)
            def _():
                fetch(nxt, 1 - slot).start()

            w16[...] = wbuf[slot].astype(BF16)
            slot_ref[0] = 1 - slot

        o_ref[...] = _dot(a_ref[...], w16[...]) + b_ref[...]

    @pl.when(b >= nu_ref[0])
    def _():
        o_ref[...] = jnp.zeros(o_ref.shape, o_ref.dtype)


def expert_down(blk_expert, nxt_expert, n_used, act, w2, b2):
    cap, d_ff = act.shape
    d = w2.shape[2]
    n_blk = cap // MOE_BLOCK

    def blk(b, nu):
        return jnp.minimum(b, nu[0] - 1)

    grid_spec = pltpu.PrefetchScalarGridSpec(
        num_scalar_prefetch=3,
        grid=(n_blk,),
        in_specs=[pl.BlockSpec((MOE_BLOCK, d_ff), lambda b, be, nx, nu: (blk(b, nu), 0)),
                  pl.BlockSpec(memory_space=pl.ANY),
                  pl.BlockSpec((None, 1, d), lambda b, be, nx, nu: (be[blk(b, nu)], 0, 0))],
        out_specs=pl.BlockSpec((MOE_BLOCK, d), lambda b, be, nx, nu: (b, 0)),
        scratch_shapes=[pltpu.VMEM((2, d_ff, d), F32),
                        pltpu.VMEM((d_ff, d), BF16),
                        pltpu.SemaphoreType.DMA((2,)),
                        pltpu.SMEM((1,), I32)],
    )
    return pl.pallas_call(
        _expert_down_kernel,
        out_shape=jax.ShapeDtypeStruct((cap, d), F32),
        grid_spec=grid_spec,
        compiler_params=_params(1),
        name="expert_down",
    )(blk_expert, nxt_expert, n_used, act, w2, b2)


def _combine_kernel(dest_ref, dest_next_ref, x_ref, gate_ref, nw_ref, yb_ref, o_ref, rows_ref, sem):
    i = pl.program_id(0)
    tt = x_ref.shape[0]
    slot = i % 2

    def row_copy(d_ref, r, k, s):
        return pltpu.make_async_copy(yb_ref.at[pl.ds(d_ref[k, r], 1), :],
                                     rows_ref.at[s, k, pl.ds(r, 1), :], sem.at[s])

    def start_tile(d_ref, s):
        def body(r, carry):
            for k in range(TOP_K):
                row_copy(d_ref, r, k, s).start(priority=k % 2)
            return carry
        lax.fori_loop(0, tt, body, 0)

    @pl.when(i == 0)
    def _():
        start_tile(dest_ref, 0)

    @pl.when(i + 1 < pl.num_programs(0))
    def _():
        start_tile(dest_next_ref, 1 - slot)

    def wait(r, carry):
        for k in range(TOP_K):
            row_copy(dest_ref, r, k, slot).wait()
        return carry

    lax.fori_loop(0, tt, wait, 0)
    gates = gate_ref[...]
    acc = x_ref[...]
    for k in range(TOP_K):
        acc = acc + gates[:, k:k + 1] * rows_ref[slot, k]
    o_ref[...] = _rms(acc, nw_ref[...])


def moe_combine(dest, x, gates, norm_w, yb, tt):
    t, d = x.shape
    last = t // tt - 1
    return pl.pallas_call(
        _combine_kernel,
        out_shape=jax.ShapeDtypeStruct((t, d), F32),
        grid=(t // tt,),
        in_specs=[pl.BlockSpec((TOP_K, tt), lambda i: (0, i), memory_space=pltpu.SMEM),
                  pl.BlockSpec((TOP_K, tt), lambda i: (0, jnp.minimum(i + 1, last)),
                               memory_space=pltpu.SMEM),
                  pl.BlockSpec((tt, d), lambda i: (i, 0)),
                  pl.BlockSpec((tt, TOP_K), lambda i: (i, 0)),
                  pl.BlockSpec((1, d), lambda i: (0, 0)),
                  pl.BlockSpec(memory_space=pl.ANY)],
        out_specs=pl.BlockSpec((tt, d), lambda i: (i, 0)),
        scratch_shapes=[pltpu.VMEM((2, TOP_K, tt, d), F32), pltpu.SemaphoreType.DMA((2,))],
        compiler_params=_params(1),
        name="moe_combine",
    )(dest, dest, x, gates, norm_w, yb)


def _lookup(table, idx):
    hit = idx[..., None] == jnp.arange(table.shape[0], dtype=I32)
    return jnp.sum(jnp.where(hit, table, 0), axis=-1)


def _layer(x, mem_b, pos_b, norm_mix_w, w_in, conv_w, conv_b, dt_bias, a_log, d_skip, ssd_norm_w,
           w_ssd_branch, w_moba_branch, w_out, norm_cross_w, norm_mem_w, w_xq, w_xkv, w_xo,
           norm_ffn_w, w_router, b_router, w_mlp1, b_mlp1, w_mlp2, b_mlp2):
    t, d = x.shape
    n_ssd_heads = dt_bias.shape[0]
    d_inner = n_ssd_heads * SSD_HEAD_DIM
    bc_width = 2 * SSD_N_GROUPS * SSD_D_STATE
    conv_dim = d_inner + bc_width
    att_width = w_moba_branch.shape[0]
    n_att_heads = att_width // ATT_HEAD_DIM
    heads_per_group = n_ssd_heads // SSD_N_GROUPS
    n_exp = w_router.shape[1]

    o_z, o_xbc = 0, d_inner
    o_dt = o_xbc + conv_dim
    o_q = o_dt + n_ssd_heads
    o_k, o_v = o_q + att_width, o_q + 2 * att_width
    o_g = o_q + 3 * att_width
    w_dt = w_in[:, o_dt:o_q].reshape(d, SSD_N_GROUPS, heads_per_group)
    w_dt = jnp.pad(w_dt, ((0, 0), (0, 0), (0, LANES - heads_per_group)))
    w_dt = w_dt.reshape(d, SSD_N_GROUPS * LANES)
    w_a = w_in[:, o_z:o_dt].astype(BF16)
    w_b = jnp.concatenate([w_in[:, o_q:], w_dt], axis=1).astype(BF16)
    c_xbc = d_inner
    c_q = c_xbc + conv_dim
    c_k, c_v = c_q + att_width, c_q + 2 * att_width
    c_g = c_q + 3 * att_width
    c_dt = c_g + 2 * d
    proj = norm_matmul2(x, norm_mix_w.reshape(1, d), w_a, w_b, tm=1024, tn=1024, out_dtype=F32)

    xbc_act = conv_silu(proj, c_xbc, conv_dim, conv_w, conv_b.reshape(1, conv_dim), tt=512, tc=1024)

    def per_group(v):
        v = v.reshape(SSD_N_GROUPS, 1, heads_per_group)
        return jnp.pad(v, ((0, 0), (0, 0), (0, LANES - heads_per_group)))

    y_ssd = ssd_scan(xbc_act, proj, c_dt // LANES, per_group(dt_bias), per_group(a_log),
                     jnp.repeat(d_skip, SSD_HEAD_DIM).reshape(1, d_inner),
                     ssd_norm_w.reshape(1, d_inner), d_inner)

    half = ATT_HEAD_DIM // 2
    inv_freq = ROPE_THETA ** (-jnp.arange(half, dtype=F32) / half)
    freq2 = jnp.concatenate([inv_freq, inv_freq]).reshape(1, ATT_HEAD_DIM)
    sign2 = jnp.concatenate([-jnp.ones((half,), F32), jnp.ones((half,), F32)]).reshape(1, ATT_HEAD_DIM)
    cos2, sin2 = rope_tables(pos_b.astype(F32).reshape(t, 1), freq2, sign2, tt=2048)
    q_rot, k_ext, v_ext, k_mean = moba_prep(proj, c_q // ATT_HEAD_DIM, c_k // ATT_HEAD_DIM,
                                            c_v // ATT_HEAD_DIM, n_att_heads, cos2, sin2, tt=2048)
    y_moba = moba_attention(q_rot, k_ext, v_ext, k_mean, qt=2048)

    mixed = gated_merge(y_ssd, y_moba, w_ssd_branch.astype(BF16), w_moba_branch.astype(BF16),
                        proj, c_g, tm=512, tn=512)
    x1 = matmul_residual(mixed, w_out.astype(BF16), x, tm=1024, tn=512)

    kv = norm_matmul(mem_b, norm_mem_w.reshape(1, d), w_xkv.astype(BF16),
                     tm=mem_b.shape[0], tn=512, out_dtype=BF16)
    x2 = cross_attention(x1, norm_cross_w.reshape(1, d), w_xq.astype(BF16), kv,
                         w_xo.astype(BF16), tm=256)

    w_rt = w_router.T
    w_rt_hi = w_rt.astype(BF16)
    w_rt_lo = (w_rt - w_rt_hi.astype(F32)).astype(BF16)
    hf, idx_t, gate_t, rank_t, counts = moe_router(x2, norm_ffn_w.reshape(1, d), w_rt_hi, w_rt_lo,
                                                   b_router.reshape(n_exp, 1), tm=512)
    counts = counts[:, 0]
    padded = (counts + MOE_BLOCK - 1) // MOE_BLOCK * MOE_BLOCK
    pad_end = jnp.cumsum(padded)
    pad_start = pad_end - padded
    dest = _lookup(pad_start, idx_t) + rank_t
    n_blk = -(-(t * TOP_K) // MOE_BLOCK) + n_exp
    cap = n_blk * MOE_BLOCK
    blk_row = jnp.arange(n_blk, dtype=I32) * MOE_BLOCK
    blk_expert = jnp.minimum(
        jnp.sum((pad_end[None, :] <= blk_row[:, None]).astype(I32), axis=1), n_exp - 1)
    n_used = pad_end[-1] // MOE_BLOCK
    run_end = _lookup(pad_end, blk_expert) // MOE_BLOCK
    nxt_expert = jnp.where(run_end < n_used,
                           _lookup(blk_expert, jnp.minimum(run_end, n_blk - 1)), -1)
    n_used = n_used.reshape(1)
    xb = moe_dispatch(pad_end, n_used, dest, hf, cap, tt=256)
    act = expert_up(blk_expert, nxt_expert, n_used, xb, w_mlp1, b_mlp1.reshape(n_exp, 1, -1),
                    tf=1024)
    yb = expert_down(blk_expert, nxt_expert, n_used, act, w_mlp2, b_mlp2.reshape(n_exp, 1, d))
    return dict(proj=proj, xbc_act=xbc_act, y_ssd=y_ssd, y_moba=y_moba, mixed=mixed, x1=x1, x2=x2,
                hf=hf, idx_t=idx_t, gate_t=gate_t, dest=dest, yb=yb)


def kernel(x, mem, positions, norm_mix_w, w_in, conv_w, conv_b, dt_bias, a_log, d_skip, ssd_norm_w,
           w_ssd_branch, w_moba_branch, w_out, norm_cross_w, norm_mem_w, w_xq, w_xkv, w_xo,
           norm_ffn_w, w_router, b_router, w_mlp1, b_mlp1, w_mlp2, b_mlp2, norm_final_w):
    assert w_in.shape[0] == 1, "depth-1 trunk only"
    d = x.shape[-1]
    outs = []
    for b in range(x.shape[0]):
        s = _layer(x[b], mem[b], positions[b], norm_mix_w[0], w_in[0], conv_w[0], conv_b[0],
                   dt_bias[0], a_log[0], d_skip[0], ssd_norm_w[0], w_ssd_branch[0],
                   w_moba_branch[0], w_out[0], norm_cross_w[0], norm_mem_w[0], w_xq[0], w_xkv[0],
                   w_xo[0], norm_ffn_w[0], w_router[0], b_router[0], w_mlp1[0], b_mlp1[0],
                   w_mlp2[0], b_mlp2[0])
        outs.append(moe_combine(s["dest"], s["x2"], s["gate_t"].T, norm_final_w.reshape(1, d),
                                s["yb"], tt=128))
    return jnp.stack(outs, axis=0)
```
